```python
import jax, jax.numpy as jnp
from jax import lax
import numpy as np

D_MODEL = 1024
BATCH = 8
SEQ = 2048
DEPTH = 4
DEC_BATCH = 1
DEC_SEQ = 16384
PAST_LEN = 128

D_MIX = D_MODEL
D_FOURIER = D_MIX // 2
N_FGROUPS = 4
F_GROUP = D_FOURIER // N_FGROUPS
D_RET = D_MIX - D_FOURIER
N_RET_HEADS = 4
RET_HEAD = D_RET // N_RET_HEADS
CHUNK = 128
D_FF = 4 * D_MODEL
D_IN = D_FOURIER + 4 * D_RET
ROPE_THETA = 10000.0
EPS = 1e-6

kernel_name = "hybrid_fnet_retnet_encoder"


def rmsnorm(x, g):
    xf = x.astype(jnp.float32)
    y = xf * lax.rsqrt(jnp.mean(xf * xf, axis=-1, keepdims=True) + EPS)
    return (y * g.astype(jnp.float32)).astype(x.dtype)


def rope(x, pos):
    half = x.shape[-1] // 2
    inv = ROPE_THETA ** (-jnp.arange(half, dtype=jnp.float32) / half)
    ang = pos[:, None] * inv[None, :]
    c = jnp.cos(ang)[None, :, None, :]
    s = jnp.sin(ang)[None, :, None, :]
    xf = x.astype(jnp.float32)
    x1, x2 = xf[..., :half], xf[..., half:]
    return jnp.concatenate([x1 * c - x2 * s, x1 * s + x2 * c], axis=-1)


def retention_chunkwise(q, k, v, log_gamma):
    B, H, S, Dk = q.shape
    Dv = v.shape[-1]
    N = S // CHUNK
    qc = q.reshape(B, H, N, CHUNK, Dk)
    kc = k.reshape(B, H, N, CHUNK, Dk)
    vc = v.reshape(B, H, N, CHUNK, Dv)
    idx = jnp.arange(CHUNK, dtype=jnp.float32)
    lg = log_gamma[:, None]
    diff = idx[:, None] - idx[None, :]
    decay_mask = jnp.where(diff[None] >= 0,
                           jnp.exp(lg[:, :, None] * jnp.maximum(diff, 0.0)[None]), 0.0)
    scores = jnp.einsum('bhnid,bhnjd->bhnij', qc, kc) * decay_mask[None, :, None]
    intra = jnp.einsum('bhnij,bhnje->bhnie', scores, vc)
    k_decay = jnp.exp(lg * (CHUNK - 1 - idx)[None, :])
    q_decay = jnp.exp(lg * (idx + 1.0)[None, :])
    chunk_decay = jnp.exp(log_gamma * CHUNK)[None, :, None, None]
    kv = jnp.einsum('bhnjd,bhnje->nbhde', kc * k_decay[None, :, None, :, None], vc)

    def step(state, kv_n):
        return state * chunk_decay + kv_n, state

    _, state_prev = lax.scan(step, jnp.zeros((B, H, Dk, Dv), jnp.float32), kv)
    cross = jnp.einsum('bhnid,nbhde->bhnie', qc * q_decay[None, :, None, :, None], state_prev)
    return (intra + cross).reshape(B, H, S, Dv)


def mixer_layer(x, ln1, w_in, w_fmix, decay_fwd, decay_bwd, gn_gain, w_o):
    B, S, _ = x.shape
    h = rmsnorm(x, ln1)
    u = h @ w_in
    o = D_FOURIER
    uf = u[..., :o]
    q = u[..., o:o + D_RET]
    k = u[..., o + D_RET:o + 2 * D_RET]
    v = u[..., o + 2 * D_RET:o + 3 * D_RET]
    g = u[..., o + 3 * D_RET:]

    ufg = uf.astype(jnp.float32).reshape(B, S, N_FGROUPS, F_GROUP)
    fr = jnp.fft.fft2(ufg, axes=(1, 3), norm="ortho").real.astype(jnp.float32)
    fo = jnp.einsum('bsgc,gcd->bsgd', fr, w_fmix.astype(jnp.float32)).reshape(B, S, D_FOURIER)

    pos = jnp.arange(S, dtype=jnp.float32)
    qh = rope(q.reshape(B, S, N_RET_HEADS, RET_HEAD), pos)
    kh = rope(k.reshape(B, S, N_RET_HEADS, RET_HEAD), pos) * (RET_HEAD ** -0.5)
    vh = v.astype(jnp.float32).reshape(B, S, N_RET_HEADS, RET_HEAD)
    qh, kh, vh = (t.transpose(0, 2, 1, 3) for t in (qh, kh, vh))
    lg_f = jax.nn.log_sigmoid(decay_fwd.astype(jnp.float32))
    lg_b = jax.nn.log_sigmoid(decay_bwd.astype(jnp.float32))
    y_f = retention_chunkwise(qh, kh, vh, lg_f)
    y_b = jnp.flip(retention_chunkwise(jnp.flip(qh, 2), jnp.flip(kh, 2), jnp.flip(vh, 2), lg_b), 2)
    y = y_f + y_b
    mu = jnp.mean(y, axis=-1, keepdims=True)
    var = jnp.mean(jnp.square(y - mu), axis=-1, keepdims=True)
    y = (y - mu) * lax.rsqrt(var + EPS)
    y = y.transpose(0, 2, 1, 3).reshape(B, S, D_RET) * gn_gain.astype(jnp.float32)
    ro = jax.nn.silu(g.astype(jnp.float32)) * y

    mix = jnp.concatenate([fo, ro], axis=-1).astype(x.dtype)
    return x + mix @ w_o


def mlp_layer(x, ln2, w_ff1, w_ff2):
    h = rmsnorm(x, ln2)
    return x + jnp.square(jax.nn.relu(h @ w_ff1)) @ w_ff2


def trunk(x, ln1, w_in, w_fmix, decay_fwd, decay_bwd, gn_gain, w_o, ln2, w_ff1, w_ff2, ln_final):
    for l in range(DEPTH):
        x = mixer_layer(x, ln1[l], w_in[l], w_fmix[l], decay_fwd[l], decay_bwd[l], gn_gain[l], w_o[l])
        x = mlp_layer(x, ln2[l], w_ff1[l], w_ff2[l])
    return rmsnorm(x, ln_final)


def setup_inputs(seed: int = 0) -> dict:
    key = jax.random.key(seed)
    ks = jax.random.split(key, 16)
    f32 = jnp.float32
    base_gamma = 1.0 - 2.0 ** (-5.0 - np.arange(N_RET_HEADS, dtype=np.float32))
    base_logit = jnp.asarray(np.log(base_gamma / (1.0 - base_gamma)), dtype=f32)
    return {
        "x_prompt": jax.random.normal(ks[0], (BATCH, SEQ, D_MODEL), f32),
        "x_sample": jax.random.normal(ks[1], (DEC_BATCH, DEC_SEQ, D_MODEL), f32),
        "ln1": 1.0 + 0.05 * jax.random.normal(ks[2], (DEPTH, D_MODEL), f32),
        "w_in": jax.random.normal(ks[3], (DEPTH, D_MODEL, D_IN), f32) * D_MODEL ** -0.5,
        "w_fmix": jax.random.normal(ks[4], (DEPTH, N_FGROUPS, F_GROUP, F_GROUP), f32) * F_GROUP ** -0.5,
        "decay_fwd": base_logit[None, :] + 0.1 * jax.random.normal(ks[5], (DEPTH, N_RET_HEADS), f32),
        "decay_bwd": base_logit[None, :] + 0.1 * jax.random.normal(ks[6], (DEPTH, N_RET_HEADS), f32),
        "gn_gain": 1.0 + 0.05 * jax.random.normal(ks[7], (DEPTH, D_RET), f32),
        "w_o": jax.random.normal(ks[8], (DEPTH, D_MIX, D_MODEL), f32) * D_MIX ** -0.5,
        "ln2": 1.0 + 0.05 * jax.random.normal(ks[9], (DEPTH, D_MODEL), f32),
        "w_ff1": jax.random.normal(ks[10], (DEPTH, D_MODEL, D_FF), f32) * D_MODEL ** -0.5,
        "w_ff2": jax.random.normal(ks[11], (DEPTH, D_FF, D_MODEL), f32) * D_FF ** -0.5,
        "ln_final": 1.0 + 0.05 * jax.random.normal(ks[12], (D_MODEL,), f32),
    }


def reference(x_prompt, x_sample, ln1, w_in, w_fmix, decay_fwd, decay_bwd, gn_gain, w_o, ln2, w_ff1, w_ff2, ln_final):
    y_prompt = trunk(x_prompt, ln1, w_in, w_fmix, decay_fwd, decay_bwd, gn_gain, w_o, ln2, w_ff1, w_ff2, ln_final)
    y_sample = trunk(x_sample, ln1, w_in, w_fmix, decay_fwd, decay_bwd, gn_gain, w_o, ln2, w_ff1, w_ff2, ln_final)
    return (y_prompt, y_sample)
```

```python
import functools

import jax
import jax.numpy as jnp
import numpy as np
from jax import lax
from jax.experimental import pallas as pl
from jax.experimental.pallas import tpu as pltpu

F32 = jnp.float32
BF16 = jnp.bfloat16

LANES = 128
SUBLANES = 8
N_HEADS = 4
D_HALF = N_HEADS * LANES
CHUNK = LANES
ROPE_THETA = 10000.0
EPS = 1e-6
VMEM_LIMIT = 56 * 1024 * 1024


def _params(n_axes=1, vmem=VMEM_LIMIT):
    return pltpu.CompilerParams(dimension_semantics=("arbitrary",) * n_axes, vmem_limit_bytes=vmem)


def _store_interleaved(ref, a, val):
    rows, n, lanes = ref.shape
    ref.reshape(rows * n, lanes)[pl.ds(a, rows, stride=n), :] = val


def _const_spec(shape):
    zeros = (0,) * len(shape)
    return pl.BlockSpec(shape, lambda *_: zeros, pipeline_mode=pl.Buffered(1))


def _fold_kernel(cs_ref, win_ref, wf_ref, wp_ref, wq_ref):
    hi = lax.Precision.HIGHEST
    m = jnp.dot(cs_ref[...], wf_ref[0, 0], precision=hi, preferred_element_type=F32)
    w = win_ref[0]
    wp_ref[0] = jnp.dot(w, m[:LANES], precision=hi, preferred_element_type=F32).astype(BF16)
    wq_ref[0] = jnp.dot(w, m[LANES:], precision=hi, preferred_element_type=F32).astype(BF16)


def _fold_fourier_weights(w_in, w_fmix):
    depth, d, _ = w_in.shape
    c = np.arange(LANES)
    ang = 2.0 * np.pi * ((c[:, None] * c[None, :]) % LANES) / LANES
    cs = jnp.asarray(np.concatenate([np.cos(ang), -np.sin(ang)], axis=0), F32)
    out = jax.ShapeDtypeStruct((depth, d, D_HALF), BF16)
    return pl.pallas_call(
        _fold_kernel,
        out_shape=(out, out),
        grid=(depth, N_HEADS),
        in_specs=[
            pl.BlockSpec((2 * LANES, LANES), lambda l, g: (0, 0)),
            pl.BlockSpec((1, d, LANES), lambda l, g: (l, 0, g)),
            pl.BlockSpec((1, 1, LANES, LANES), lambda l, g: (l, g, 0, 0)),
        ],
        out_specs=(
            pl.BlockSpec((1, d, LANES), lambda l, g: (l, 0, g)),
            pl.BlockSpec((1, d, LANES), lambda l, g: (l, 0, g)),
        ),
        compiler_params=_params(2),
        name="fold_fourier_weights",
    )(cs, w_in, w_fmix)


def _rmsnorm(x, gain):
    return x * lax.rsqrt(jnp.mean(x * x, axis=-1, keepdims=True) + EPS) * gain


def _proj_kernel(x_ref, ln_ref, wp_ref, wq_ref, wr_ref, cos_ref, sin_ref,
                 z_ref, q_ref, kt_ref, v_ref, g_ref, *, slabs):
    tm = x_ref.shape[0]
    h = _rmsnorm(x_ref[...], ln_ref[...]).astype(BF16)
    zr = jnp.dot(h, wp_ref[...], preferred_element_type=F32)
    zi = jnp.dot(h, wq_ref[...], preferred_element_type=F32)
    if slabs:
        n_s1 = tm // LANES
        for half, val in enumerate((zr, zi)):
            for sl in range(N_HEADS):
                for a in range(n_s1):
                    _store_interleaved(z_ref.at[half * N_HEADS + sl], a,
                                       val[a * LANES:(a + 1) * LANES, sl * LANES:(sl + 1) * LANES])
    else:
        z_ref[:, :D_HALF] = zr
        z_ref[:, D_HALF:] = zi

    cos = cos_ref[...]
    sin = sin_ref[...]

    def rope(t):
        return t * cos + pltpu.roll(t, LANES // 2, 1) * sin

    qf = jnp.dot(h, wr_ref[:, 0:D_HALF], preferred_element_type=F32)
    kf = jnp.dot(h, wr_ref[:, D_HALF:2 * D_HALF], preferred_element_type=F32)
    k_scale = LANES ** -0.5
    for hd in range(N_HEADS):
        cols = slice(hd * LANES, (hd + 1) * LANES)
        q_ref[:, cols] = rope(qf[:, cols]).astype(BF16)
        kt = (rope(kf[:, cols]) * k_scale).T.astype(BF16)
        for c in range(tm // CHUNK):
            kt_ref[c, cols, :] = kt[:, c * CHUNK:(c + 1) * CHUNK]
    v_ref[...] = jnp.dot(h, wr_ref[:, 2 * D_HALF:3 * D_HALF], preferred_element_type=F32).astype(BF16)
    g_ref[...] = jnp.dot(h, wr_ref[:, 3 * D_HALF:], preferred_element_type=F32)


def _proj(x, ln, wp, wq, wr, cos_t, sin_t, *, seq, slabs):
    t, d = x.shape
    tm = 1024 if slabs else 512
    pos_blocks = seq // tm
    n_chunks = tm // CHUNK
    if slabs:
        assert t == seq and tm // LANES == SUBLANES
        z_shape = jax.ShapeDtypeStruct((2 * N_HEADS, LANES, seq // LANES, LANES), F32)
        z_spec = pl.BlockSpec((2 * N_HEADS, LANES, tm // LANES, LANES), lambda i: (0, 0, i, 0))
    else:
        z_shape = jax.ShapeDtypeStruct((t, 2 * D_HALF), F32)
        z_spec = pl.BlockSpec((tm, 2 * D_HALF), lambda i: (i, 0))
    row = lambda i: (i, 0)
    return pl.pallas_call(
        functools.partial(_proj_kernel, slabs=slabs),
        out_shape=(
            z_shape,
            jax.ShapeDtypeStruct((t, D_HALF), BF16),
            jax.ShapeDtypeStruct((t // CHUNK, D_HALF, CHUNK), BF16),
            jax.ShapeDtypeStruct((t, D_HALF), BF16),
            jax.ShapeDtypeStruct((t, D_HALF), F32),
        ),
        grid=(t // tm,),
        in_specs=[
            pl.BlockSpec((tm, d), row),
            _const_spec((1, d)),
            _const_spec((d, D_HALF)),
            _const_spec((d, D_HALF)),
            _const_spec((d, 4 * D_HALF)),
            pl.BlockSpec((tm, LANES), lambda i: (i % pos_blocks, 0)),
            pl.BlockSpec((tm, LANES), lambda i: (i % pos_blocks, 0)),
        ],
        out_specs=(
            z_spec,
            pl.BlockSpec((tm, D_HALF), row),
            pl.BlockSpec((n_chunks, D_HALF, CHUNK), lambda i: (i, 0, 0)),
            pl.BlockSpec((tm, D_HALF), row),
            pl.BlockSpec((tm, D_HALF), row),
        ),
        compiler_params=_params(1),
        name="proj_slabs" if slabs else "proj",
    )(x, ln, wp, wq, wr, cos_t, sin_t)


def _log_sigmoid(x):
    return -(jnp.maximum(-x, 0.0) + jnp.log1p(jnp.exp(-jnp.abs(x))))


def _ret_back_kernel(dec_ref, kt_ref, v_ref, sb_ref, carry_ref, *, blocks_per_seq):
    n_chunks = kt_ref.shape[0]
    blk = pl.num_programs(0) - 1 - pl.program_id(0)

    @pl.when(blk % blocks_per_seq == blocks_per_seq - 1)
    def _():
        carry_ref[...] = jnp.zeros_like(carry_ref)

    lg = _log_sigmoid(dec_ref[...])
    lane = lax.broadcasted_iota(jnp.int32, (SUBLANES, LANES), 1).astype(F32)
    k_decay = [jnp.exp(lg[hd] * lane)[0:1, :] for hd in range(N_HEADS)]
    c_decay = [jnp.exp(lg[hd] * float(CHUNK))[0:1, :] for hd in range(N_HEADS)]

    def body(i, carry):
        c = n_chunks - 1 - i
        rows = pl.ds(pl.multiple_of(c * CHUNK, CHUNK), CHUNK)
        for hd in range(N_HEADS):
            cols = slice(hd * LANES, (hd + 1) * LANES)
            state = carry_ref[hd]
            sb_ref[c, hd] = state.astype(BF16)
            kd = (kt_ref[c, cols, :].astype(F32) * k_decay[hd]).astype(BF16)
            carry_ref[hd] = state * c_decay[hd] + jnp.dot(kd, v_ref[rows, cols], preferred_element_type=F32)
        return carry

    lax.fori_loop(0, n_chunks, body, 0)


def _ret_back(dec_b, kt, v, *, seq, blk):
    t = v.shape[0]
    nb = t // blk
    n_chunks = blk // CHUNK
    rev = lambda j: nb - 1 - j
    return pl.pallas_call(
        functools.partial(_ret_back_kernel, blocks_per_seq=seq // blk),
        out_shape=jax.ShapeDtypeStruct((t // CHUNK, N_HEADS, LANES, LANES), BF16),
        grid=(nb,),
        in_specs=[
            _const_spec((N_HEADS, SUBLANES, LANES)),
            pl.BlockSpec((n_chunks, D_HALF, CHUNK), lambda j: (rev(j), 0, 0)),
            pl.BlockSpec((blk, D_HALF), lambda j: (rev(j), 0)),
        ],
        out_specs=pl.BlockSpec((n_chunks, N_HEADS, LANES, LANES), lambda j: (rev(j), 0, 0, 0)),
        scratch_shapes=[pltpu.VMEM((N_HEADS, LANES, LANES), F32)],
        compiler_params=_params(1),
        name="ret_back",
    )(dec_b, kt, v)


def _ret_main_kernel(decf_ref, decb_ref, gain_ref, q_ref, kt_ref, v_ref, g_ref, sb_ref, o_ref,
                     sf_ref, mask_ref, qdf_ref, qdb_ref, kdf_ref, cdf_ref, *, blocks_per_seq):
    n_chunks = kt_ref.shape[0]
    blk = pl.program_id(0)

    @pl.when(blk == 0)
    def _():
        lgf = _log_sigmoid(decf_ref[...])
        lgb = _log_sigmoid(decb_ref[...])
        i = lax.broadcasted_iota(jnp.int32, (CHUNK, CHUNK), 0).astype(F32)
        j = lax.broadcasted_iota(jnp.int32, (CHUNK, CHUNK), 1).astype(F32)
        lane = lax.broadcasted_iota(jnp.int32, (SUBLANES, LANES), 1).astype(F32)
        diff = i - j
        for hd in range(N_HEADS):
            lf = lgf[hd][0:1, :]
            lb = lgb[hd][0:1, :]
            fwd = jnp.where(diff >= 0, jnp.exp(lf * jnp.maximum(diff, 0.0)), 0.0)
            bwd = jnp.where(diff <= 0, jnp.exp(lb * jnp.maximum(-diff, 0.0)), 0.0)
            mask_ref[hd] = fwd + bwd
            qdf_ref[hd] = jnp.exp(lf * (i + 1.0))
            qdb_ref[hd] = jnp.exp(lb * (float(CHUNK) - i))
            kdf_ref[hd] = jnp.exp(lgf[hd] * (float(CHUNK - 1) - lane))
            cdf_ref[hd] = jnp.exp(lgf[hd] * float(CHUNK))

    @pl.when(blk % blocks_per_seq == 0)
    def _():
        sf_ref[...] = jnp.zeros_like(sf_ref)

    def body(c, carry):
        rows = pl.ds(pl.multiple_of(c * CHUNK, CHUNK), CHUNK)
        for hd in range(N_HEADS):
            cols = slice(hd * LANES, (hd + 1) * LANES)
            qh = q_ref[rows, cols]
            kth = kt_ref[c, cols, :]
            vh = v_ref[rows, cols]
            sf = sf_ref[hd]
            scores = jnp.dot(qh, kth, preferred_element_type=F32) * mask_ref[hd]
            qf = qh.astype(F32)
            lhs = jnp.concatenate(
                [scores.astype(BF16), (qf * qdf_ref[hd]).astype(BF16), (qf * qdb_ref[hd]).astype(BF16)], axis=1)
            rhs = jnp.concatenate([vh, sf.astype(BF16), sb_ref[c, hd]], axis=0)
            y = jnp.dot(lhs, rhs, preferred_element_type=F32)
            mu = jnp.mean(y, axis=-1, keepdims=True)
            d = y - mu
            var = jnp.mean(d * d, axis=-1, keepdims=True)
            yn = d * lax.rsqrt(var + EPS) * gain_ref[:, cols]
            gate = g_ref[rows, cols]
            o_ref[rows, cols] = (gate * jax.nn.sigmoid(gate) * yn).astype(BF16)
            kd = (kth.astype(F32) * kdf_ref[hd][0:1, :]).astype(BF16)
            sf_ref[hd] = sf * cdf_ref[hd][0:1, :] + jnp.dot(kd, vh, preferred_element_type=F32)
        return carry

    lax.fori_loop(0, n_chunks, body, 0)


def _ret_main(dec_f, dec_b, gain, q, kt, v, g, sb, *, seq, blk):
    t = v.shape[0]
    n_chunks = blk // CHUNK
    row = lambda j: (j, 0)
    tab = pltpu.VMEM((N_HEADS, CHUNK, CHUNK), F32)
    vec = pltpu.VMEM((N_HEADS, SUBLANES, LANES), F32)
    return pl.pallas_call(
        functools.partial(_ret_main_kernel, blocks_per_seq=seq // blk),
        out_shape=jax.ShapeDtypeStruct((t, D_HALF), BF16),
        grid=(t // blk,),
        in_specs=[
            _const_spec((N_HEADS, SUBLANES, LANES)),
            _const_spec((N_HEADS, SUBLANES, LANES)),
            _const_spec((1, D_HALF)),
            pl.BlockSpec((blk, D_HALF), row),
            pl.BlockSpec((n_chunks, D_HALF, CHUNK), lambda j: (j, 0, 0)),
            pl.BlockSpec((blk, D_HALF), row),
            pl.BlockSpec((blk, D_HALF), row),
            pl.BlockSpec((n_chunks, N_HEADS, LANES, LANES), lambda j: (j, 0, 0, 0)),
        ],
        out_specs=pl.BlockSpec((blk, D_HALF), row),
        scratch_shapes=[tab, tab, tab, tab, vec, vec],
        compiler_params=_params(1),
        name="ret_main",
    )(dec_f, dec_b, gain, q, kt, v, g, sb)


def _dft_dense_kernel(a_ref, z_ref, o_ref):
    s = z_ref.shape[0]
    zr = z_ref[:, :D_HALF].astype(BF16)
    zi = z_ref[:, D_HALF:].astype(BF16)
    out = jnp.dot(a_ref[:, :s], zr, preferred_element_type=F32)
    out += jnp.dot(a_ref[:, s:], zi, preferred_element_type=F32)
    o_ref[...] = out.astype(o_ref.dtype)


def _dft_dense(z, *, seq):
    t = z.shape[0]
    k = lax.broadcasted_iota(jnp.int32, (seq, seq), 0)
    s = lax.broadcasted_iota(jnp.int32, (seq, seq), 1)
    ang = ((k * s) % seq).astype(F32) * (2.0 * np.pi / seq)
    scale = (seq * LANES) ** -0.5
    a = (jnp.concatenate([jnp.cos(ang), jnp.sin(ang)], axis=1) * scale).astype(BF16)
    halves = 2
    return pl.pallas_call(
        _dft_dense_kernel,
        out_shape=jax.ShapeDtypeStruct((t, D_HALF), BF16),
        grid=(halves, t // seq),
        in_specs=[
            pl.BlockSpec((seq // halves, 2 * seq), lambda m, b: (m, 0)),
            pl.BlockSpec((seq, 2 * D_HALF), lambda m, b: (b, 0)),
        ],
        out_specs=pl.BlockSpec((seq // halves, D_HALF), lambda m, b: (b * halves + m, 0)),
        compiler_params=_params(2),
        name="dft_dense",
    )(a, z)


def _dft_stage_a_kernel(z_ref, g_ref, y_ref):
    n1 = z_ref.shape[2]
    for jj in range(SUBLANES):
        zr = jnp.concatenate([z_ref[sl, jj] for sl in range(N_HEADS)], axis=1)
        zi = jnp.concatenate([z_ref[N_HEADS + sl, jj] for sl in range(N_HEADS)], axis=1)
        rhs = jnp.concatenate([zr, zi], axis=0).astype(BF16)
        y = jnp.dot(g_ref[jj], rhs, preferred_element_type=F32)
        for part in range(2):
            for sl in range(N_HEADS):
                _store_interleaved(y_ref.at[part, sl], jj, y[part * n1:(part + 1) * n1, sl * LANES:(sl + 1) * LANES])


def _dft_stage_b_kernel(y_ref, h_ref, o_ref):
    n2 = y_ref.shape[3]
    for kk in range(SUBLANES):
        yr = jnp.concatenate([y_ref[0, sl, kk] for sl in range(N_HEADS)], axis=1)
        yi = jnp.concatenate([y_ref[1, sl, kk] for sl in range(N_HEADS)], axis=1)
        rhs = jnp.concatenate([yr, yi], axis=0).astype(BF16)
        out = jnp.dot(h_ref[...], rhs, preferred_element_type=F32)
        for sl in range(N_HEADS):
            _store_interleaved(o_ref.at[sl], kk, out[:, sl * LANES:(sl + 1) * LANES])


def _dft_two_stage(z, *, seq):
    n2 = LANES
    n1 = seq // n2
    s2 = lax.broadcasted_iota(jnp.int32, (n2, n1, n1), 0)
    k1 = lax.broadcasted_iota(jnp.int32, (n2, n1, n1), 1)
    s1 = lax.broadcasted_iota(jnp.int32, (n2, n1, n1), 2)
    ang = ((k1 * (n2 * s1 + s2)) % seq).astype(F32) * (2.0 * np.pi / seq)
    c, s = jnp.cos(ang), jnp.sin(ang)
    g = jnp.concatenate([jnp.concatenate([c, s], axis=2), jnp.concatenate([-s, c], axis=2)], axis=1).astype(BF16)
    k2 = lax.broadcasted_iota(jnp.int32, (n2, n2), 0)
    t2 = lax.broadcasted_iota(jnp.int32, (n2, n2), 1)
    ang2 = ((k2 * t2) % n2).astype(F32) * (2.0 * np.pi / n2)
    scale = (seq * LANES) ** -0.5
    hm = (jnp.concatenate([jnp.cos(ang2), jnp.sin(ang2)], axis=1) * scale).astype(BF16)

    y = pl.pallas_call(
        _dft_stage_a_kernel,
        out_shape=jax.ShapeDtypeStruct((2, N_HEADS, n1, n2, LANES), F32),
        grid=(n2 // SUBLANES,),
        in_specs=[
            pl.BlockSpec((2 * N_HEADS, SUBLANES, n1, LANES), lambda j: (0, j, 0, 0)),
            pl.BlockSpec((SUBLANES, 2 * n1, 2 * n1), lambda j: (j, 0, 0)),
        ],
        out_specs=pl.BlockSpec((2, N_HEADS, n1, SUBLANES, LANES), lambda j: (0, 0, 0, j, 0)),
        compiler_params=_params(1),
        name="dft_stage_a",
    )(z, g)
    fo = pl.pallas_call(
        _dft_stage_b_kernel,
        out_shape=jax.ShapeDtypeStruct((N_HEADS, n2, n1, LANES), F32),
        grid=(n1 // SUBLANES,),
        in_specs=[
            pl.BlockSpec((2, N_HEADS, SUBLANES, n2, LANES), lambda i: (0, 0, i, 0, 0)),
            _const_spec((n2, 2 * n2)),
        ],
        out_specs=pl.BlockSpec((N_HEADS, n2, SUBLANES, LANES), lambda i: (0, 0, i, 0)),
        compiler_params=_params(1),
        name="dft_stage_b",
    )(y, hm)
    return fo.reshape(N_HEADS, seq, LANES)


FF_CHUNK = 1024


def _out_mlp_kernel(x_ref, fo_ref, ro_ref, wo_ref, ln2_ref, w1_ref, w2_ref, lnf_ref, o_ref, *, slabs, final):
    if slabs:
        fo = jnp.concatenate([fo_ref[sl] for sl in range(N_HEADS)], axis=1).astype(BF16)
    else:
        fo = fo_ref[...]
    x1 = x_ref[...] + jnp.dot(fo, wo_ref[:D_HALF], preferred_element_type=F32)
    x1 = x1 + jnp.dot(ro_ref[...], wo_ref[D_HALF:], preferred_element_type=F32)
    h = _rmsnorm(x1, ln2_ref[...]).astype(BF16)
    acc = None
    for c in range(w1_ref.shape[1] // FF_CHUNK):
        cols = slice(c * FF_CHUNK, (c + 1) * FF_CHUNK)
        a = jnp.dot(h, w1_ref[:, cols], preferred_element_type=F32)
        a = jnp.square(jnp.maximum(a, 0.0)).astype(BF16)
        part = jnp.dot(a, w2_ref[cols, :], preferred_element_type=F32)
        acc = part if acc is None else acc + part
    acc = acc + x1
    if final:
        acc = _rmsnorm(acc, lnf_ref[...])
    o_ref[...] = acc


def _out_mlp(x, fo, ro, wo, ln2, w1, w2, lnf, *, slabs, final):
    t, d = x.shape
    tm = 512
    row = lambda i: (i, 0)
    if slabs:
        fo_spec = pl.BlockSpec((N_HEADS, tm, LANES), lambda i: (0, i, 0))
    else:
        fo_spec = pl.BlockSpec((tm, D_HALF), row)
    return pl.pallas_call(
        functools.partial(_out_mlp_kernel, slabs=slabs, final=final),
        out_shape=jax.ShapeDtypeStruct((t, d), F32),
        grid=(t // tm,),
        in_specs=[
            pl.BlockSpec((tm, d), row),
            fo_spec,
            pl.BlockSpec((tm, D_HALF), row),
            _const_spec(wo.shape),
            _const_spec((1, d)),
            _const_spec(w1.shape),
            _const_spec(w2.shape),
            _const_spec((1, d)),
        ],
        out_specs=pl.BlockSpec((tm, d), row),
        compiler_params=_params(1),
        name="out_mlp",
    )(x, fo, ro, wo, ln2, w1, w2, lnf)


def _rope_tables(seq):
    half = LANES // 2
    inv = ROPE_THETA ** (-np.arange(half, dtype=np.float64) / half)
    ang = np.arange(seq, dtype=np.float64)[:, None] * inv[None, :]
    c, s = np.cos(ang), np.sin(ang)
    cos_t = np.concatenate([c, c], axis=1).astype(np.float32)
    sin_t = np.concatenate([-s, s], axis=1).astype(np.float32)
    return jnp.asarray(cos_t), jnp.asarray(sin_t)


def _bcast_heads(v):
    return jnp.broadcast_to(v.astype(F32)[:, None, None], (N_HEADS, SUBLANES, LANES))


def _trunk(x, weights, *, seq, two_stage):
    wp, wq, wr, wo, w1, w2, ln1, ln2, dec_f, dec_b, gain, lnf = weights
    depth = wp.shape[0]
    cos_t, sin_t = _rope_tables(seq)
    blk = min(seq, 2048)
    for l in range(depth):
        z, q, kt, v, g = _proj(x, ln1[l][None], wp[l], wq[l], wr[l], cos_t, sin_t, seq=seq, slabs=two_stage)
        df, db = _bcast_heads(dec_f[l]), _bcast_heads(dec_b[l])
        sb = _ret_back(db, kt, v, seq=seq, blk=blk)
        ro = _ret_main(df, db, gain[l][None].astype(F32), q, kt, v, g, sb, seq=seq, blk=blk)
        fo = _dft_two_stage(z, seq=seq) if two_stage else _dft_dense(z, seq=seq)
        x = _out_mlp(x, fo, ro, wo[l], ln2[l][None], w1[l], w2[l], lnf[None],
                     slabs=two_stage, final=(l == depth - 1))
    return x


def kernel(x_prompt, x_sample, ln1, w_in, w_fmix, decay_fwd, decay_bwd, gn_gain, w_o, ln2, w_ff1, w_ff2, ln_final):
    d = x_prompt.shape[-1]
    wp, wq = _fold_fourier_weights(w_in, w_fmix)
    weights = (wp, wq, w_in[:, :, D_HALF:].astype(BF16), w_o.astype(BF16), w_ff1.astype(BF16), w_ff2.astype(BF16),
               ln1, ln2, decay_fwd, decay_bwd, gn_gain, ln_final)
    y_prompt = _trunk(x_prompt.reshape(-1, d), weights, seq=x_prompt.shape[1], two_stage=False)
    y_sample = _trunk(x_sample.reshape(-1, d), weights, seq=x_sample.shape[1], two_stage=True)
    return y_prompt.reshape(x_prompt.shape), y_sample.reshape(x_sample.shape)
```

```python
import functools

import jax
import jax.numpy as jnp
import numpy as np
from jax import lax
from jax.experimental import pallas as pl
from jax.experimental.pallas import tpu as pltpu

F32 = jnp.float32
BF16 = jnp.bfloat16

LANES = 128
SUBLANES = 8
N_HEADS = 4
D_HALF = N_HEADS * LANES
CHUNK = LANES
ROPE_THETA = 10000.0
EPS = 1e-6
VMEM_LIMIT = 56 * 1024 * 1024


def _params(n_axes=1, vmem=VMEM_LIMIT):
    return pltpu.CompilerParams(dimension_semantics=("arbitrary",) * n_axes, vmem_limit_bytes=vmem)


def _store_interleaved(ref, a, val):
    rows, n, lanes = ref.shape
    ref.reshape(rows * n, lanes)[pl.ds(a, rows, stride=n), :] = val


def _const_spec(shape):
    zeros = (0,) * len(shape)
    return pl.BlockSpec(shape, lambda *_: zeros, pipeline_mode=pl.Buffered(1))


N_PROJ_BLOCKS = 6 * N_HEADS
PROJ_Q, PROJ_K, PROJ_V, PROJ_G = (b * D_HALF for b in (2, 3, 4, 5))


def _prep_proj_kernel(cs_ref, win_ref, wf_ref, o_ref):
    n = pl.program_id(1)
    hi = lax.Precision.HIGHEST
    w = win_ref[0]

    @pl.when(n < 2 * N_HEADS)
    def _():
        half = pl.multiple_of((n // N_HEADS) * LANES, LANES)
        m = jnp.dot(cs_ref[pl.ds(half, LANES), :], wf_ref[0, 0], precision=hi, preferred_element_type=F32)
        o_ref[0] = jnp.dot(w, m, precision=hi, preferred_element_type=F32).astype(BF16)

    @pl.when(jnp.logical_and(n >= 4 * N_HEADS, n < 5 * N_HEADS))
    def _():
        o_ref[0] = (w - jnp.mean(w, axis=-1, keepdims=True)).astype(BF16)

    @pl.when(jnp.logical_or(jnp.logical_and(n >= 2 * N_HEADS, n < 4 * N_HEADS), n >= 5 * N_HEADS))
    def _():
        o_ref[0] = w.astype(BF16)


def _prep_proj_weights(w_in, w_fmix):
    depth, d, _ = w_in.shape
    c = np.arange(LANES)
    ang = 2.0 * np.pi * ((c[:, None] * c[None, :]) % LANES) / LANES
    cs = jnp.asarray(np.concatenate([np.cos(ang), -np.sin(ang)], axis=0), F32)
    src_block = lambda n: jnp.where(n < 2 * N_HEADS, n % N_HEADS, n - N_HEADS)
    return pl.pallas_call(
        _prep_proj_kernel,
        out_shape=jax.ShapeDtypeStruct((depth, d, N_PROJ_BLOCKS * LANES), BF16),
        grid=(depth, N_PROJ_BLOCKS),
        in_specs=[
            pl.BlockSpec((2 * LANES, LANES), lambda l, n: (0, 0)),
            pl.BlockSpec((1, d, LANES), lambda l, n: (l, 0, src_block(n))),
            pl.BlockSpec((1, 1, LANES, LANES), lambda l, n: (l, n % N_HEADS, 0, 0)),
        ],
        out_specs=pl.BlockSpec((1, d, LANES), lambda l, n: (l, 0, n)),
        compiler_params=_params(2),
        name="prep_proj_weights",
    )(cs, w_in, w_fmix)


def _rmsnorm(x, gain):
    return x * lax.rsqrt(jnp.mean(x * x, axis=-1, keepdims=True) + EPS) * gain


def _proj_kernel(x_ref, ln_ref, w_ref, cos_ref, sin_ref, z_ref, q_ref, kt_ref, v_ref, g_ref, *, slabs):
    tm = x_ref.shape[0]
    h = _rmsnorm(x_ref[...], ln_ref[...]).astype(BF16)

    def project(first_col):
        return jnp.dot(h, w_ref[:, first_col:first_col + D_HALF], preferred_element_type=F32)

    zr = project(0)
    zi = project(D_HALF)
    if slabs:
        n_s1 = tm // LANES
        for half, val in enumerate((zr, zi)):
            for sl in range(N_HEADS):
                for a in range(n_s1):
                    _store_interleaved(z_ref.at[half * N_HEADS + sl], a,
                                       val[a * LANES:(a + 1) * LANES, sl * LANES:(sl + 1) * LANES])
    else:
        z_ref[:, :D_HALF] = zr
        z_ref[:, D_HALF:] = zi

    cos = cos_ref[...]
    sin = sin_ref[...]

    def rope(t):
        return t * cos + pltpu.roll(t, LANES // 2, 1) * sin

    qf = project(PROJ_Q)
    kf = project(PROJ_K)
    k_scale = LANES ** -0.5
    for hd in range(N_HEADS):
        cols = slice(hd * LANES, (hd + 1) * LANES)
        q_ref[:, cols] = rope(qf[:, cols]).astype(BF16)
        kt = (rope(kf[:, cols]) * k_scale).T.astype(BF16)
        for c in range(tm // CHUNK):
            kt_ref[c, cols, :] = kt[:, c * CHUNK:(c + 1) * CHUNK]
    v_ref[...] = project(PROJ_V).astype(BF16)
    g_ref[...] = project(PROJ_G)


def _proj(x, ln, w, cos_t, sin_t, *, seq, slabs):
    t, d = x.shape
    tm = 1024 if slabs else 512
    pos_blocks = seq // tm
    n_chunks = tm // CHUNK
    if slabs:
        assert t == seq and tm // LANES == SUBLANES
        z_shape = jax.ShapeDtypeStruct((2 * N_HEADS, LANES, seq // LANES, LANES), F32)
        z_spec = pl.BlockSpec((2 * N_HEADS, LANES, tm // LANES, LANES), lambda i: (0, 0, i, 0))
    else:
        z_shape = jax.ShapeDtypeStruct((t, 2 * D_HALF), F32)
        z_spec = pl.BlockSpec((tm, 2 * D_HALF), lambda i: (i, 0))
    row = lambda i: (i, 0)
    return pl.pallas_call(
        functools.partial(_proj_kernel, slabs=slabs),
        out_shape=(
            z_shape,
            jax.ShapeDtypeStruct((t, D_HALF), BF16),
            jax.ShapeDtypeStruct((t // CHUNK, D_HALF, CHUNK), BF16),
            jax.ShapeDtypeStruct((t, D_HALF), BF16),
            jax.ShapeDtypeStruct((t, D_HALF), F32),
        ),
        grid=(t // tm,),
        in_specs=[
            pl.BlockSpec((tm, d), row),
            _const_spec((1, d)),
            _const_spec(w.shape),
            pl.BlockSpec((tm, LANES), lambda i: (i % pos_blocks, 0)),
            pl.BlockSpec((tm, LANES), lambda i: (i % pos_blocks, 0)),
        ],
        out_specs=(
            z_spec,
            pl.BlockSpec((tm, D_HALF), row),
            pl.BlockSpec((n_chunks, D_HALF, CHUNK), lambda i: (i, 0, 0)),
            pl.BlockSpec((tm, D_HALF), row),
            pl.BlockSpec((tm, D_HALF), row),
        ),
        compiler_params=_params(1),
        name="proj_slabs" if slabs else "proj",
    )(x, ln, w, cos_t, sin_t)


def _log_sigmoid(x):
    return -(jnp.maximum(-x, 0.0) + jnp.log1p(jnp.exp(-jnp.abs(x))))


def _ret_states_kernel(decf_ref, decb_ref, ktf_ref, vf_ref, ktb_ref, vb_ref, sf_ref, sb_ref, cf_ref, cb_ref,
                       *, blocks_per_seq):
    n_chunks = ktf_ref.shape[0]
    blk_f = pl.program_id(0)
    blk_b = pl.num_programs(0) - 1 - blk_f

    @pl.when(blk_f % blocks_per_seq == 0)
    def _():
        cf_ref[...] = jnp.zeros_like(cf_ref)

    @pl.when(blk_b % blocks_per_seq == blocks_per_seq - 1)
    def _():
        cb_ref[...] = jnp.zeros_like(cb_ref)

    lgf = _log_sigmoid(decf_ref[...])
    lgb = _log_sigmoid(decb_ref[...])
    lane = lax.broadcasted_iota(jnp.int32, (SUBLANES, LANES), 1).astype(F32)
    kdf = [jnp.exp(lgf[hd] * (float(CHUNK - 1) - lane))[0:1, :] for hd in range(N_HEADS)]
    kdb = [jnp.exp(lgb[hd] * lane)[0:1, :] for hd in range(N_HEADS)]
    cdf = [jnp.exp(lgf[hd] * float(CHUNK))[0:1, :] for hd in range(N_HEADS)]
    cdb = [jnp.exp(lgb[hd] * float(CHUNK))[0:1, :] for hd in range(N_HEADS)]

    def scan_step(c, kt_ref, v_ref, out_ref, carry_ref, k_decay, c_decay):
        rows = pl.ds(pl.multiple_of(c * CHUNK, CHUNK), CHUNK)
        for hd in range(N_HEADS):
            cols = slice(hd * LANES, (hd + 1) * LANES)
            state = carry_ref[hd]
            out_ref[c, hd] = state.astype(BF16)
            kd = (kt_ref[c, cols, :].astype(F32) * k_decay[hd]).astype(BF16)
            carry_ref[hd] = state * c_decay[hd] + jnp.dot(kd, v_ref[rows, cols], preferred_element_type=F32)

    def body(i, carry):
        scan_step(i, ktf_ref, vf_ref, sf_ref, cf_ref, kdf, cdf)
        scan_step(n_chunks - 1 - i, ktb_ref, vb_ref, sb_ref, cb_ref, kdb, cdb)
        return carry

    lax.fori_loop(0, n_chunks, body, 0)


def _ret_states(dec_f, dec_b, kt, v, *, seq, blk):
    t = v.shape[0]
    nb = t // blk
    n_chunks = blk // CHUNK
    rev = lambda j: nb - 1 - j
    state = jax.ShapeDtypeStruct((t // CHUNK, N_HEADS, LANES, LANES), BF16)
    carry = pltpu.VMEM((N_HEADS, LANES, LANES), F32)
    return pl.pallas_call(
        functools.partial(_ret_states_kernel, blocks_per_seq=seq // blk),
        out_shape=(state, state),
        grid=(nb,),
        in_specs=[
            _const_spec((N_HEADS, SUBLANES, LANES)),
            _const_spec((N_HEADS, SUBLANES, LANES)),
            pl.BlockSpec((n_chunks, D_HALF, CHUNK), lambda j: (j, 0, 0)),
            pl.BlockSpec((blk, D_HALF), lambda j: (j, 0)),
            pl.BlockSpec((n_chunks, D_HALF, CHUNK), lambda j: (rev(j), 0, 0)),
            pl.BlockSpec((blk, D_HALF), lambda j: (rev(j), 0)),
        ],
        out_specs=(
            pl.BlockSpec((n_chunks, N_HEADS, LANES, LANES), lambda j: (j, 0, 0, 0)),
            pl.BlockSpec((n_chunks, N_HEADS, LANES, LANES), lambda j: (rev(j), 0, 0, 0)),
        ),
        scratch_shapes=[carry, carry],
        compiler_params=_params(1),
        name="ret_states",
    )(dec_f, dec_b, kt, v, kt, v)


RET_UNROLL = 2


def _ret_main_kernel(decf_ref, decb_ref, gain_ref, q_ref, kt_ref, v_ref, g_ref, sf_ref, sb_ref, o_ref,
                     mask_ref, qdf_ref, qdb_ref, gs_ref):
    n_chunks = kt_ref.shape[0]

    @pl.when(pl.program_id(0) == 0)
    def _():
        lgf = _log_sigmoid(decf_ref[...])
        lgb = _log_sigmoid(decb_ref[...])
        i = lax.broadcasted_iota(jnp.int32, (CHUNK, CHUNK), 0).astype(F32)
        j = lax.broadcasted_iota(jnp.int32, (CHUNK, CHUNK), 1).astype(F32)
        diff = i - j
        for hd in range(N_HEADS):
            lf = lgf[hd][0:1, :]
            lb = lgb[hd][0:1, :]
            fwd = jnp.where(diff >= 0, jnp.exp(lf * jnp.maximum(diff, 0.0)), 0.0)
            bwd = jnp.where(diff <= 0, jnp.exp(lb * jnp.maximum(-diff, 0.0)), 0.0)
            mask_ref[hd] = fwd + bwd
            qdf_ref[hd] = jnp.exp(lf * (i + 1.0)).astype(BF16)
            qdb_ref[hd] = jnp.exp(lb * (float(CHUNK) - i)).astype(BF16)
        gs_ref[...] = gain_ref[...] * (float(LANES) ** 0.5)

    def one_chunk(c):
        rows = pl.ds(pl.multiple_of(c * CHUNK, CHUNK), CHUNK)
        for hd in range(N_HEADS):
            cols = slice(hd * LANES, (hd + 1) * LANES)
            qh = q_ref[rows, cols]
            scores = jnp.dot(qh, kt_ref[c, cols, :], preferred_element_type=F32) * mask_ref[hd]
            lhs = jnp.concatenate([scores.astype(BF16), qh * qdf_ref[hd], qh * qdb_ref[hd]], axis=1)
            rhs = jnp.concatenate([v_ref[rows, cols], sf_ref[c, hd], sb_ref[c, hd]], axis=0)
            d = jnp.dot(lhs, rhs, preferred_element_type=F32)
            ss = jnp.sum(d * d, axis=-1, keepdims=True)
            yn = d * lax.rsqrt(ss + float(LANES) * EPS) * gs_ref[:, cols]
            gate = g_ref[rows, cols]
            o_ref[rows, cols] = (gate * jax.nn.sigmoid(gate) * yn).astype(BF16)

    def body(i, carry):
        for u in range(RET_UNROLL):
            one_chunk(i * RET_UNROLL + u)
        return carry

    lax.fori_loop(0, n_chunks // RET_UNROLL, body, 0)


def _ret_main(dec_f, dec_b, gain, q, kt, v, g, sf, sb, *, blk):
    t = v.shape[0]
    n_chunks = blk // CHUNK
    assert n_chunks % RET_UNROLL == 0
    row = lambda j: (j, 0)
    state_spec = pl.BlockSpec((n_chunks, N_HEADS, LANES, LANES), lambda j: (j, 0, 0, 0))
    return pl.pallas_call(
        _ret_main_kernel,
        out_shape=jax.ShapeDtypeStruct((t, D_HALF), BF16),
        grid=(t // blk,),
        in_specs=[
            _const_spec((N_HEADS, SUBLANES, LANES)),
            _const_spec((N_HEADS, SUBLANES, LANES)),
            _const_spec((1, D_HALF)),
            pl.BlockSpec((blk, D_HALF), row),
            pl.BlockSpec((n_chunks, D_HALF, CHUNK), lambda j: (j, 0, 0)),
            pl.BlockSpec((blk, D_HALF), row),
            pl.BlockSpec((blk, D_HALF), row),
            state_spec,
            state_spec,
        ],
        out_specs=pl.BlockSpec((blk, D_HALF), row),
        scratch_shapes=[
            pltpu.VMEM((N_HEADS, CHUNK, CHUNK), F32),
            pltpu.VMEM((N_HEADS, CHUNK, CHUNK), BF16),
            pltpu.VMEM((N_HEADS, CHUNK, CHUNK), BF16),
            pltpu.VMEM((1, D_HALF), F32),
        ],
        compiler_params=_params(1),
        name="ret_main",
    )(dec_f, dec_b, gain, q, kt, v, g, sf, sb)


def _dft_dense_kernel(a_ref, z_ref, o_ref):
    s = z_ref.shape[0]
    zr = z_ref[:, :D_HALF].astype(BF16)
    zi = z_ref[:, D_HALF:].astype(BF16)
    out = jnp.dot(a_ref[:, :s], zr, preferred_element_type=F32)
    out += jnp.dot(a_ref[:, s:], zi, preferred_element_type=F32)
    o_ref[...] = out.astype(o_ref.dtype)


def _dft_dense(z, *, seq):
    t = z.shape[0]
    k = lax.broadcasted_iota(jnp.int32, (seq, seq), 0)
    s = lax.broadcasted_iota(jnp.int32, (seq, seq), 1)
    ang = ((k * s) % seq).astype(F32) * (2.0 * np.pi / seq)
    scale = (seq * LANES) ** -0.5
    a = (jnp.concatenate([jnp.cos(ang), jnp.sin(ang)], axis=1) * scale).astype(BF16)
    halves = 2
    return pl.pallas_call(
        _dft_dense_kernel,
        out_shape=jax.ShapeDtypeStruct((t, D_HALF), BF16),
        grid=(halves, t // seq),
        in_specs=[
            pl.BlockSpec((seq // halves, 2 * seq), lambda m, b: (m, 0)),
            pl.BlockSpec((seq, 2 * D_HALF), lambda m, b: (b, 0)),
        ],
        out_specs=pl.BlockSpec((seq // halves, D_HALF), lambda m, b: (b * halves + m, 0)),
        compiler_params=_params(2),
        name="dft_dense",
    )(a, z)


def _dft_stage_a_kernel(z_ref, g_ref, y_ref):
    n1 = z_ref.shape[2]
    for jj in range(SUBLANES):
        zr = jnp.concatenate([z_ref[sl, jj] for sl in range(N_HEADS)], axis=1)
        zi = jnp.concatenate([z_ref[N_HEADS + sl, jj] for sl in range(N_HEADS)], axis=1)
        rhs = jnp.concatenate([zr, zi], axis=0).astype(BF16)
        y = jnp.dot(g_ref[jj], rhs, preferred_element_type=F32)
        for part in range(2):
            for sl in range(N_HEADS):
                _store_interleaved(y_ref.at[part, sl], jj, y[part * n1:(part + 1) * n1, sl * LANES:(sl + 1) * LANES])


def _dft_stage_b_kernel(y_ref, h_ref, o_ref):
    n2 = y_ref.shape[3]
    for kk in range(SUBLANES):
        yr = jnp.concatenate([y_ref[0, sl, kk] for sl in range(N_HEADS)], axis=1)
        yi = jnp.concatenate([y_ref[1, sl, kk] for sl in range(N_HEADS)], axis=1)
        rhs = jnp.concatenate([yr, yi], axis=0).astype(BF16)
        out = jnp.dot(h_ref[...], rhs, preferred_element_type=F32)
        for sl in range(N_HEADS):
            _store_interleaved(o_ref.at[sl], kk, out[:, sl * LANES:(sl + 1) * LANES])


def _dft_two_stage(z, *, seq):
    n2 = LANES
    n1 = seq // n2
    s2 = lax.broadcasted_iota(jnp.int32, (n2, n1, n1), 0)
    k1 = lax.broadcasted_iota(jnp.int32, (n2, n1, n1), 1)
    s1 = lax.broadcasted_iota(jnp.int32, (n2, n1, n1), 2)
    ang = ((k1 * (n2 * s1 + s2)) % seq).astype(F32) * (2.0 * np.pi / seq)
    c, s = jnp.cos(ang), jnp.sin(ang)
    g = jnp.concatenate([jnp.concatenate([c, s], axis=2), jnp.concatenate([-s, c], axis=2)], axis=1).astype(BF16)
    k2 = lax.broadcasted_iota(jnp.int32, (n2, n2), 0)
    t2 = lax.broadcasted_iota(jnp.int32, (n2, n2), 1)
    ang2 = ((k2 * t2) % n2).astype(F32) * (2.0 * np.pi / n2)
    scale = (seq * LANES) ** -0.5
    hm = (jnp.concatenate([jnp.cos(ang2), jnp.sin(ang2)], axis=1) * scale).astype(BF16)

    y = pl.pallas_call(
        _dft_stage_a_kernel,
        out_shape=jax.ShapeDtypeStruct((2, N_HEADS, n1, n2, LANES), F32),
        grid=(n2 // SUBLANES,),
        in_specs=[
            pl.BlockSpec((2 * N_HEADS, SUBLANES, n1, LANES), lambda j: (0, j, 0, 0)),
            pl.BlockSpec((SUBLANES, 2 * n1, 2 * n1), lambda j: (j, 0, 0)),
        ],
        out_specs=pl.BlockSpec((2, N_HEADS, n1, SUBLANES, LANES), lambda j: (0, 0, 0, j, 0)),
        compiler_params=_params(1),
        name="dft_stage_a",
    )(z, g)
    fo = pl.pallas_call(
        _dft_stage_b_kernel,
        out_shape=jax.ShapeDtypeStruct((N_HEADS, n2, n1, LANES), F32),
        grid=(n1 // SUBLANES,),
        in_specs=[
            pl.BlockSpec((2, N_HEADS, SUBLANES, n2, LANES), lambda i: (0, 0, i, 0, 0)),
            _const_spec((n2, 2 * n2)),
        ],
        out_specs=pl.BlockSpec((N_HEADS, n2, SUBLANES, LANES), lambda i: (0, 0, i, 0)),
        compiler_params=_params(1),
        name="dft_stage_b",
    )(y, hm)
    return fo.reshape(N_HEADS, seq, LANES)


FF_CHUNK = 1024


def _out_mlp_kernel(x_ref, fo_ref, ro_ref, wo_ref, ln2_ref, w1_ref, w2_ref, lnf_ref, o_ref, *, slabs, final):
    if slabs:
        fo = jnp.concatenate([fo_ref[sl] for sl in range(N_HEADS)], axis=1).astype(BF16)
    else:
        fo = fo_ref[...]
    x1 = x_ref[...] + jnp.dot(fo, wo_ref[:D_HALF], preferred_element_type=F32)
    x1 = x1 + jnp.dot(ro_ref[...], wo_ref[D_HALF:], preferred_element_type=F32)
    h = _rmsnorm(x1, ln2_ref[...]).astype(BF16)
    acc = None
    for c in range(w1_ref.shape[1] // FF_CHUNK):
        cols = slice(c * FF_CHUNK, (c + 1) * FF_CHUNK)
        a = jnp.dot(h, w1_ref[:, cols], preferred_element_type=F32)
        a = jnp.square(jnp.maximum(a, 0.0)).astype(BF16)
        part = jnp.dot(a, w2_ref[cols, :], preferred_element_type=F32)
        acc = part if acc is None else acc + part
    acc = acc + x1
    if final:
        acc = _rmsnorm(acc, lnf_ref[...])
    o_ref[...] = acc


def _out_mlp(x, fo, ro, wo, ln2, w1, w2, lnf, *, slabs, final):
    t, d = x.shape
    tm = 512
    row = lambda i: (i, 0)
    if slabs:
        fo_spec = pl.BlockSpec((N_HEADS, tm, LANES), lambda i: (0, i, 0))
    else:
        fo_spec = pl.BlockSpec((tm, D_HALF), row)
    return pl.pallas_call(
        functools.partial(_out_mlp_kernel, slabs=slabs, final=final),
        out_shape=jax.ShapeDtypeStruct((t, d), F32),
        grid=(t // tm,),
        in_specs=[
            pl.BlockSpec((tm, d), row),
            fo_spec,
            pl.BlockSpec((tm, D_HALF), row),
            _const_spec(wo.shape),
            _const_spec((1, d)),
            _const_spec(w1.shape),
            _const_spec(w2.shape),
            _const_spec((1, d)),
        ],
        out_specs=pl.BlockSpec((tm, d), row),
        compiler_params=_params(1),
        name="out_mlp",
    )(x, fo, ro, wo, ln2, w1, w2, lnf)


def _rope_tables(seq):
    half = LANES // 2
    inv = ROPE_THETA ** (-np.arange(half, dtype=np.float64) / half)
    ang = np.arange(seq, dtype=np.float64)[:, None] * inv[None, :]
    c, s = np.cos(ang), np.sin(ang)
    cos_t = np.concatenate([c, c], axis=1).astype(np.float32)
    sin_t = np.concatenate([-s, s], axis=1).astype(np.float32)
    return jnp.asarray(cos_t), jnp.asarray(sin_t)


def _bcast_heads(v):
    return jnp.broadcast_to(v.astype(F32)[:, None, None], (N_HEADS, SUBLANES, LANES))


def _trunk(x, weights, *, seq, two_stage):
    wproj, wo, w1, w2, ln1, ln2, dec_f, dec_b, gain, lnf = weights
    depth = wproj.shape[0]
    cos_t, sin_t = _rope_tables(seq)
    blk = min(seq, 2048)
    for l in range(depth):
        z, q, kt, v, g = _proj(x, ln1[l][None], wproj[l], cos_t, sin_t, seq=seq, slabs=two_stage)
        df, db = _bcast_heads(dec_f[l]), _bcast_heads(dec_b[l])
        sf, sb = _ret_states(df, db, kt, v, seq=seq, blk=blk)
        ro = _ret_main(df, db, gain[l][None].astype(F32), q, kt, v, g, sf, sb, blk=blk)
        fo = _dft_two_stage(z, seq=seq) if two_stage else _dft_dense(z, seq=seq)
        x = _out_mlp(x, fo, ro, wo[l], ln2[l][None], w1[l], w2[l], lnf[None],
                     slabs=two_stage, final=(l == depth - 1))
    return x


def kernel(x_prompt, x_sample, ln1, w_in, w_fmix, decay_fwd, decay_bwd, gn_gain, w_o, ln2, w_ff1, w_ff2, ln_final):
    d = x_prompt.shape[-1]
    weights = (_prep_proj_weights(w_in, w_fmix), w_o.astype(BF16), w_ff1.astype(BF16), w_ff2.astype(BF16),
               ln1, ln2, decay_fwd, decay_bwd, gn_gain, ln_final)
    y_prompt = _trunk(x_prompt.reshape(-1, d), weights, seq=x_prompt.shape[1], two_stage=False)
    y_sample = _trunk(x_sample.reshape(-1, d), weights, seq=x_sample.shape[1], two_stage=True)
    return y_prompt.reshape(x_prompt.shape), y_sample.reshape(x_sample.shape)
```

```python
import functools

import jax
import jax.numpy as jnp
import numpy as np
from jax import lax
from jax.experimental import pallas as pl
from jax.experimental.pallas import tpu as pltpu

F32 = jnp.float32
BF16 = jnp.bfloat16

LANES = 128
SUBLANES = 8
N_HEADS = 4
D_HALF = N_HEADS * LANES
CHUNK = LANES
ROPE_THETA = 10000.0
EPS = 1e-6
VMEM_LIMIT = 56 * 1024 * 1024


def _params(n_axes=1, vmem=VMEM_LIMIT):
    return pltpu.CompilerParams(dimension_semantics=("arbitrary",) * n_axes, vmem_limit_bytes=vmem)


def _store_interleaved(ref, a, val):
    rows, n, lanes = ref.shape
    ref.reshape(rows * n, lanes)[pl.ds(a, rows, stride=n), :] = val


def _const_spec(shape):
    zeros = (0,) * len(shape)
    return pl.BlockSpec(shape, lambda *_: zeros, pipeline_mode=pl.Buffered(1))


def _layer_spec(stacked_shape, layer):
    index = (layer,) + (0,) * (len(stacked_shape) - 1)
    return pl.BlockSpec((None,) + tuple(stacked_shape[1:]), lambda *_: index, pipeline_mode=pl.Buffered(1))


def _pack_pair(lo, hi):
    return pltpu.pack_elementwise([lo, hi], packed_dtype=BF16)


def _unpack_pair(words):
    lo = pltpu.unpack_elementwise(words, index=0, packed_dtype=BF16, unpacked_dtype=F32)
    hi = pltpu.unpack_elementwise(words, index=1, packed_dtype=BF16, unpacked_dtype=F32)
    return lo, hi


PROJ_ZR, PROJ_ZI, PROJ_Q, PROJ_K, PROJ_V, PROJ_G = (b * D_HALF for b in range(6))


def _prep_proj_kernel(cs_ref, win_ref, wf_ref, o_ref):
    n = pl.program_id(1)
    hi = lax.Precision.HIGHEST
    groups = [slice(g * LANES, (g + 1) * LANES) for g in range(N_HEADS)]

    @pl.when(n < 2)
    def _():
        cs = cs_ref[pl.ds(pl.multiple_of(n * LANES, LANES), LANES), :]
        for g, cols in enumerate(groups):
            m = jnp.dot(cs, wf_ref[0, g], precision=hi, preferred_element_type=F32)
            o_ref[0, :, cols] = jnp.dot(win_ref[0, :, cols], m, precision=hi, preferred_element_type=F32).astype(BF16)

    @pl.when(n == PROJ_V // D_HALF)
    def _():
        for cols in groups:
            w = win_ref[0, :, cols]
            o_ref[0, :, cols] = (w - jnp.mean(w, axis=-1, keepdims=True)).astype(BF16)

    @pl.when(jnp.logical_and(n >= 2, n != PROJ_V // D_HALF))
    def _():
        o_ref[0] = win_ref[0].astype(BF16)


def _prep_proj_weights(w_in, w_fmix):
    depth, d, _ = w_in.shape
    c = np.arange(LANES)
    ang = 2.0 * np.pi * ((c[:, None] * c[None, :]) % LANES) / LANES
    cs = jnp.asarray(np.concatenate([np.cos(ang), -np.sin(ang)], axis=0), F32)
    n_blocks = 6
    return pl.pallas_call(
        _prep_proj_kernel,
        out_shape=jax.ShapeDtypeStruct((depth, d, n_blocks * D_HALF), BF16),
        grid=(depth, n_blocks),
        in_specs=[
            pl.BlockSpec((2 * LANES, LANES), lambda l, n: (0, 0)),
            pl.BlockSpec((1, d, D_HALF), lambda l, n: (l, 0, jnp.maximum(n - 1, 0))),
            pl.BlockSpec((1, N_HEADS, LANES, LANES), lambda l, n: (l, 0, 0, 0)),
        ],
        out_specs=pl.BlockSpec((1, d, D_HALF), lambda l, n: (l, 0, n)),
        compiler_params=_params(2),
        name="prep_proj_weights",
    )(cs, w_in, w_fmix)


def _rmsnorm(x, gain):
    return x * lax.rsqrt(jnp.mean(x * x, axis=-1, keepdims=True) + EPS) * gain


def _proj_kernel(x_ref, ln_ref, w_ref, cos_ref, sin_ref, z_ref, q_ref, kt_ref, v_ref, g_ref):
    tm = x_ref.shape[0]
    h = _rmsnorm(x_ref[...], ln_ref[...]).astype(BF16)

    def project(first_col):
        return jnp.dot(h, w_ref[:, first_col:first_col + D_HALF], preferred_element_type=F32)

    zw = _pack_pair(project(PROJ_ZR), project(PROJ_ZI))
    for sl in range(N_HEADS):
        for a in range(tm // LANES):
            _store_interleaved(z_ref.at[sl], a, zw[a * LANES:(a + 1) * LANES, sl * LANES:(sl + 1) * LANES])

    cos = cos_ref[...]
    sin = sin_ref[...]

    def rope(t):
        return t * cos + pltpu.roll(t, LANES // 2, 1) * sin

    qf = project(PROJ_Q)
    kf = project(PROJ_K)
    k_scale = LANES ** -0.5
    for hd in range(N_HEADS):
        cols = slice(hd * LANES, (hd + 1) * LANES)
        q_ref[:, cols] = rope(qf[:, cols]).astype(BF16)
        kt = (rope(kf[:, cols]) * k_scale).T.astype(BF16)
        for c in range(tm // CHUNK):
            kt_ref[c, cols, :] = kt[:, c * CHUNK:(c + 1) * CHUNK]
    v_ref[...] = project(PROJ_V).astype(BF16)
    g_ref[...] = project(PROJ_G)


def _proj(x, ln, w, layer, cos_t, sin_t, *, seq):
    t, d = x.shape
    tm = SUBLANES * LANES
    pos_blocks = seq // tm
    n_chunks = tm // CHUNK
    row = lambda i: (i, 0)
    return pl.pallas_call(
        _proj_kernel,
        out_shape=(
            jax.ShapeDtypeStruct((N_HEADS, LANES, t // LANES, LANES), jnp.uint32),
            jax.ShapeDtypeStruct((t, D_HALF), BF16),
            jax.ShapeDtypeStruct((t // CHUNK, D_HALF, CHUNK), BF16),
            jax.ShapeDtypeStruct((t, D_HALF), BF16),
            jax.ShapeDtypeStruct((t, D_HALF), F32),
        ),
        grid=(t // tm,),
        in_specs=[
            pl.BlockSpec((tm, d), row),
            _const_spec((1, d)),
            _layer_spec(w.shape, layer),
            pl.BlockSpec((tm, LANES), lambda i: (i % pos_blocks, 0)),
            pl.BlockSpec((tm, LANES), lambda i: (i % pos_blocks, 0)),
        ],
        out_specs=(
            pl.BlockSpec((N_HEADS, LANES, tm // LANES, LANES), lambda i: (0, 0, i, 0)),
            pl.BlockSpec((tm, D_HALF), row),
            pl.BlockSpec((n_chunks, D_HALF, CHUNK), lambda i: (i, 0, 0)),
            pl.BlockSpec((tm, D_HALF), row),
            pl.BlockSpec((tm, D_HALF), row),
        ),
        compiler_params=_params(1),
        name="proj",
    )(x, ln, w, cos_t, sin_t)


def _log_sigmoid(x):
    return -(jnp.maximum(-x, 0.0) + jnp.log1p(jnp.exp(-jnp.abs(x))))


def _ret_states_kernel(decf_ref, decb_ref, ktf_ref, vf_ref, ktb_ref, vb_ref, sf_ref, sb_ref, cf_ref, cb_ref,
                       *, blocks_per_seq):
    n_chunks = ktf_ref.shape[0]
    blk_f = pl.program_id(0)
    blk_b = pl.num_programs(0) - 1 - blk_f

    @pl.when(blk_f % blocks_per_seq == 0)
    def _():
        cf_ref[...] = jnp.zeros_like(cf_ref)

    @pl.when(blk_b % blocks_per_seq == blocks_per_seq - 1)
    def _():
        cb_ref[...] = jnp.zeros_like(cb_ref)

    lgf = _log_sigmoid(decf_ref[...])
    lgb = _log_sigmoid(decb_ref[...])
    lane = lax.broadcasted_iota(jnp.int32, (SUBLANES, LANES), 1).astype(F32)
    kdf = [jnp.exp(lgf[hd] * (float(CHUNK - 1) - lane))[0:1, :] for hd in range(N_HEADS)]
    kdb = [jnp.exp(lgb[hd] * lane)[0:1, :] for hd in range(N_HEADS)]
    cdf = [jnp.exp(lgf[hd] * float(CHUNK))[0:1, :] for hd in range(N_HEADS)]
    cdb = [jnp.exp(lgb[hd] * float(CHUNK))[0:1, :] for hd in range(N_HEADS)]

    def scan_step(c, kt_ref, v_ref, out_ref, carry_ref, k_decay, c_decay):
        rows = pl.ds(pl.multiple_of(c * CHUNK, CHUNK), CHUNK)
        for hd in range(N_HEADS):
            cols = slice(hd * LANES, (hd + 1) * LANES)
            state = carry_ref[hd]
            out_ref[c, hd] = state.astype(BF16)
            kd = (kt_ref[c, cols, :].astype(F32) * k_decay[hd]).astype(BF16)
            carry_ref[hd] = state * c_decay[hd] + jnp.dot(kd, v_ref[rows, cols], preferred_element_type=F32)

    def body(i, carry):
        scan_step(i, ktf_ref, vf_ref, sf_ref, cf_ref, kdf, cdf)
        scan_step(n_chunks - 1 - i, ktb_ref, vb_ref, sb_ref, cb_ref, kdb, cdb)
        return carry

    lax.fori_loop(0, n_chunks, body, 0)


def _ret_states(dec_f, dec_b, kt, v, *, seq, blk):
    t = v.shape[0]
    nb = t // blk
    n_chunks = blk // CHUNK
    rev = lambda j: nb - 1 - j
    state = jax.ShapeDtypeStruct((t // CHUNK, N_HEADS, LANES, LANES), BF16)
    carry = pltpu.VMEM((N_HEADS, LANES, LANES), F32)
    return pl.pallas_call(
        functools.partial(_ret_states_kernel, blocks_per_seq=seq // blk),
        out_shape=(state, state),
        grid=(nb,),
        in_specs=[
            _const_spec((N_HEADS, SUBLANES, LANES)),
            _const_spec((N_HEADS, SUBLANES, LANES)),
            pl.BlockSpec((n_chunks, D_HALF, CHUNK), lambda j: (j, 0, 0)),
            pl.BlockSpec((blk, D_HALF), lambda j: (j, 0)),
            pl.BlockSpec((n_chunks, D_HALF, CHUNK), lambda j: (rev(j), 0, 0)),
            pl.BlockSpec((blk, D_HALF), lambda j: (rev(j), 0)),
        ],
        out_specs=(
            pl.BlockSpec((n_chunks, N_HEADS, LANES, LANES), lambda j: (j, 0, 0, 0)),
            pl.BlockSpec((n_chunks, N_HEADS, LANES, LANES), lambda j: (rev(j), 0, 0, 0)),
        ),
        scratch_shapes=[carry, carry],
        compiler_params=_params(1),
        name="ret_states",
    )(dec_f, dec_b, kt, v, kt, v)


RET_UNROLL = 2


def _ret_main_kernel(decf_ref, decb_ref, gain_ref, q_ref, kt_ref, v_ref, g_ref, sf_ref, sb_ref, o_ref,
                     mask_ref, qdf_ref, qdb_ref, gs_ref):
    n_chunks = kt_ref.shape[0]

    @pl.when(pl.program_id(0) == 0)
    def _():
        lgf = _log_sigmoid(decf_ref[...])
        lgb = _log_sigmoid(decb_ref[...])
        i = lax.broadcasted_iota(jnp.int32, (CHUNK, CHUNK), 0).astype(F32)
        j = lax.broadcasted_iota(jnp.int32, (CHUNK, CHUNK), 1).astype(F32)
        diff = i - j
        for hd in range(N_HEADS):
            lf = lgf[hd][0:1, :]
            lb = lgb[hd][0:1, :]
            fwd = jnp.where(diff >= 0, jnp.exp(lf * jnp.maximum(diff, 0.0)), 0.0)
            bwd = jnp.where(diff <= 0, jnp.exp(lb * jnp.maximum(-diff, 0.0)), 0.0)
            mask_ref[hd] = fwd + bwd
            qdf_ref[hd] = jnp.exp(lf * (i + 1.0)).astype(BF16)
            qdb_ref[hd] = jnp.exp(lb * (float(CHUNK) - i)).astype(BF16)
        gs_ref[...] = gain_ref[...] * (float(LANES) ** 0.5)

    def one_chunk(c):
        rows = pl.ds(pl.multiple_of(c * CHUNK, CHUNK), CHUNK)
        for hd in range(N_HEADS):
            cols = slice(hd * LANES, (hd + 1) * LANES)
            qh = q_ref[rows, cols]
            scores = jnp.dot(qh, kt_ref[c, cols, :], preferred_element_type=F32) * mask_ref[hd]
            lhs = jnp.concatenate([scores.astype(BF16), qh * qdf_ref[hd], qh * qdb_ref[hd]], axis=1)
            rhs = jnp.concatenate([v_ref[rows, cols], sf_ref[c, hd], sb_ref[c, hd]], axis=0)
            d = jnp.dot(lhs, rhs, preferred_element_type=F32)
            ss = jnp.sum(d * d, axis=-1, keepdims=True)
            yn = d * lax.rsqrt(ss + float(LANES) * EPS) * gs_ref[:, cols]
            gate = g_ref[rows, cols]
            o_ref[rows, cols] = (gate * jax.nn.sigmoid(gate) * yn).astype(BF16)

    def body(i, carry):
        for u in range(RET_UNROLL):
            one_chunk(i * RET_UNROLL + u)
        return carry

    lax.fori_loop(0, n_chunks // RET_UNROLL, body, 0)


def _ret_main(dec_f, dec_b, gain, q, kt, v, g, sf, sb, *, blk):
    t = v.shape[0]
    n_chunks = blk // CHUNK
    assert n_chunks % RET_UNROLL == 0
    row = lambda j: (j, 0)
    state_spec = pl.BlockSpec((n_chunks, N_HEADS, LANES, LANES), lambda j: (j, 0, 0, 0))
    return pl.pallas_call(
        _ret_main_kernel,
        out_shape=jax.ShapeDtypeStruct((t, D_HALF), BF16),
        grid=(t // blk,),
        in_specs=[
            _const_spec((N_HEADS, SUBLANES, LANES)),
            _const_spec((N_HEADS, SUBLANES, LANES)),
            _const_spec((1, D_HALF)),
            pl.BlockSpec((blk, D_HALF), row),
            pl.BlockSpec((n_chunks, D_HALF, CHUNK), lambda j: (j, 0, 0)),
            pl.BlockSpec((blk, D_HALF), row),
            pl.BlockSpec((blk, D_HALF), row),
            state_spec,
            state_spec,
        ],
        out_specs=pl.BlockSpec((blk, D_HALF), row),
        scratch_shapes=[
            pltpu.VMEM((N_HEADS, CHUNK, CHUNK), F32),
            pltpu.VMEM((N_HEADS, CHUNK, CHUNK), BF16),
            pltpu.VMEM((N_HEADS, CHUNK, CHUNK), BF16),
            pltpu.VMEM((1, D_HALF), F32),
        ],
        compiler_params=_params(1),
        name="ret_main",
    )(dec_f, dec_b, gain, q, kt, v, g, sf, sb)


def _slab_row(ref, *index):
    return jnp.concatenate([ref[(sl,) + index] for sl in range(N_HEADS)], axis=1)


def _dft_stage_a_kernel(z_ref, g_ref, y_ref):
    r = z_ref.shape[2]
    for jj in range(SUBLANES):
        zr, zi = _unpack_pair(_slab_row(z_ref, jj))
        rhs = jnp.concatenate([zr, zi], axis=0).astype(BF16)
        y = jnp.dot(g_ref[jj], rhs, preferred_element_type=F32)
        yw = _pack_pair(y[:r], y[r:])
        for sl in range(N_HEADS):
            _store_interleaved(y_ref.at[sl], jj, yw[:, sl * LANES:(sl + 1) * LANES])


def _dft_stage_b_kernel(y_ref, h_ref, o_ref):
    for kk in range(SUBLANES):
        yr, yi = _unpack_pair(_slab_row(y_ref, kk))
        rhs = jnp.concatenate([yr, yi], axis=0).astype(BF16)
        out = jnp.dot(h_ref[...], rhs, preferred_element_type=F32)
        for sl in range(N_HEADS):
            _store_interleaved(o_ref.at[sl], kk, out[:, sl * LANES:(sl + 1) * LANES])


def _dft_tables(seq, batch):
    n2 = LANES
    n1 = seq // n2
    idx = np.arange(n1)
    ang_a = 2.0 * np.pi * ((idx[:, None] * idx[None, :]) % n1) / n1
    ang_b = 2.0 * np.pi * (np.arange(n2)[:, None] * idx[None, :]) / seq
    ca, sa = jnp.asarray(np.cos(ang_a), F32)[None], jnp.asarray(np.sin(ang_a), F32)[None]
    cb, sb = jnp.asarray(np.cos(ang_b), F32)[:, :, None], jnp.asarray(np.sin(ang_b), F32)[:, :, None]
    c = ca * cb - sa * sb
    s = sa * cb + ca * sb
    eye = jnp.eye(batch, dtype=F32)[None, :, None, :, None]

    def per_sequence(m):
        return (eye * m[:, None, :, None, :]).reshape(n2, batch * n1, batch * n1)

    g = jnp.concatenate([jnp.concatenate([per_sequence(c), per_sequence(s)], axis=2),
                         jnp.concatenate([per_sequence(-s), per_sequence(c)], axis=2)], axis=1).astype(BF16)
    k2 = np.arange(n2)
    ang2 = 2.0 * np.pi * ((k2[:, None] * k2[None, :]) % n2) / n2
    scale = (seq * LANES) ** -0.5
    hm = jnp.asarray(np.concatenate([np.cos(ang2), np.sin(ang2)], axis=1) * scale, F32).astype(BF16)
    return g, hm


def _dft_two_stage(z, g, hm):
    _, n2, r, _ = z.shape
    y = pl.pallas_call(
        _dft_stage_a_kernel,
        out_shape=jax.ShapeDtypeStruct((N_HEADS, r, n2, LANES), jnp.uint32),
        grid=(n2 // SUBLANES,),
        in_specs=[
            pl.BlockSpec((N_HEADS, SUBLANES, r, LANES), lambda j: (0, j, 0, 0)),
            pl.BlockSpec((SUBLANES, 2 * r, 2 * r), lambda j: (j, 0, 0)),
        ],
        out_specs=pl.BlockSpec((N_HEADS, r, SUBLANES, LANES), lambda j: (0, 0, j, 0)),
        compiler_params=_params(1),
        name="dft_stage_a",
    )(z, g)
    return pl.pallas_call(
        _dft_stage_b_kernel,
        out_shape=jax.ShapeDtypeStruct((N_HEADS, n2, r, LANES), F32),
        grid=(r // SUBLANES,),
        in_specs=[
            pl.BlockSpec((N_HEADS, SUBLANES, n2, LANES), lambda i: (0, i, 0, 0)),
            _const_spec((n2, 2 * n2)),
        ],
        out_specs=pl.BlockSpec((N_HEADS, n2, SUBLANES, LANES), lambda i: (0, 0, i, 0)),
        compiler_params=_params(1),
        name="dft_stage_b",
    )(y, hm)


FF_CHUNK = 1024


def _out_mlp_kernel(x_ref, fo_ref, ro_ref, wo_ref, ln2_ref, w1_ref, w2_ref, lnf_ref, o_ref, *, final):
    tm = x_ref.shape[0]
    fo = jnp.concatenate([fo_ref[sl].reshape(tm, LANES) for sl in range(N_HEADS)], axis=1).astype(BF16)
    x1 = x_ref[...] + jnp.dot(fo, wo_ref[:D_HALF], preferred_element_type=F32)
    x1 = x1 + jnp.dot(ro_ref[...], wo_ref[D_HALF:], preferred_element_type=F32)
    h = _rmsnorm(x1, ln2_ref[...]).astype(BF16)
    acc = None
    for c in range(w1_ref.shape[1] // FF_CHUNK):
        cols = slice(c * FF_CHUNK, (c + 1) * FF_CHUNK)
        a = jnp.dot(h, w1_ref[:, cols], preferred_element_type=F32)
        a = jnp.square(jnp.maximum(a, 0.0)).astype(BF16)
        part = jnp.dot(a, w2_ref[cols, :], preferred_element_type=F32)
        acc = part if acc is None else acc + part
    acc = acc + x1
    if final:
        acc = _rmsnorm(acc, lnf_ref[...])
    o_ref[...] = acc


def _out_mlp(x, fo, ro, wo, ln2, w1, w2, lnf, layer, *, seq, final):
    t, d = x.shape
    tm = 512
    n1 = seq // LANES
    tiles_per_seq = seq // tm
    fo = fo.reshape(N_HEADS, LANES, t // seq, n1, LANES)
    row = lambda i: (i, 0)
    return pl.pallas_call(
        functools.partial(_out_mlp_kernel, final=final),
        out_shape=jax.ShapeDtypeStruct((t, d), F32),
        grid=(t // tm,),
        in_specs=[
            pl.BlockSpec((tm, d), row),
            pl.BlockSpec((N_HEADS, tm // n1, None, n1, LANES), lambda i: (0, i % tiles_per_seq, i // tiles_per_seq, 0, 0)),
            pl.BlockSpec((tm, D_HALF), row),
            _layer_spec(wo.shape, layer),
            _const_spec((1, d)),
            _layer_spec(w1.shape, layer),
            _layer_spec(w2.shape, layer),
            _const_spec((1, d)),
        ],
        out_specs=pl.BlockSpec((tm, d), row),
        compiler_params=_params(1),
        name="out_mlp",
    )(x, fo, ro, wo, ln2, w1, w2, lnf)


def _rope_tables(seq):
    half = LANES // 2
    inv = ROPE_THETA ** (-np.arange(half, dtype=np.float64) / half)
    ang = np.arange(seq, dtype=np.float64)[:, None] * inv[None, :]
    c, s = np.cos(ang), np.sin(ang)
    cos_t = np.concatenate([c, c], axis=1).astype(np.float32)
    sin_t = np.concatenate([-s, s], axis=1).astype(np.float32)
    return jnp.asarray(cos_t), jnp.asarray(sin_t)


def _bcast_heads(v):
    return jnp.broadcast_to(v.astype(F32)[:, None, None], (N_HEADS, SUBLANES, LANES))


def _trunk(x, weights, *, seq):
    wproj, wo, w1, w2, ln1, ln2, dec_f, dec_b, gain, lnf = weights
    depth = wproj.shape[0]
    cos_t, sin_t = _rope_tables(seq)
    g_dft, h_dft = _dft_tables(seq, x.shape[0] // seq)
    blk = min(seq, 2048)
    for l in range(depth):
        z, q, kt, v, g = _proj(x, ln1[l][None], wproj, l, cos_t, sin_t, seq=seq)
        df, db = _bcast_heads(dec_f[l]), _bcast_heads(dec_b[l])
        sf, sb = _ret_states(df, db, kt, v, seq=seq, blk=blk)
        ro = _ret_main(df, db, gain[l][None].astype(F32), q, kt, v, g, sf, sb, blk=blk)
        fo = _dft_two_stage(z, g_dft, h_dft)
        x = _out_mlp(x, fo, ro, wo, ln2[l][None], w1, w2, lnf[None], l, seq=seq, final=(l == depth - 1))
    return x


def kernel(x_prompt, x_sample, ln1, w_in, w_fmix, decay_fwd, decay_bwd, gn_gain, w_o, ln2, w_ff1, w_ff2, ln_final):
    d = x_prompt.shape[-1]
    weights = (_prep_proj_weights(w_in, w_fmix), w_o.astype(BF16), w_ff1.astype(BF16), w_ff2.astype(BF16),
               ln1, ln2, decay_fwd, decay_bwd, gn_gain, ln_final)
    y_prompt = _trunk(x_prompt.reshape(-1, d), weights, seq=x_prompt.shape[1])
    y_sample = _trunk(x_sample.reshape(-1, d), weights, seq=x_sample.shape[1])
    return y_prompt.reshape(x_prompt.shape), y_sample.reshape(x_sample.shape)
```

```python
import functools

import jax
import jax.numpy as jnp
import numpy as np
from jax import lax
from jax.experimental import pallas as pl
from jax.experimental.pallas import tpu as pltpu

F32 = jnp.float32
BF16 = jnp.bfloat16

LANES = 128
SUBLANES = 8
N_HEADS = 4
D_HALF = N_HEADS * LANES
CHUNK = LANES
ROPE_THETA = 10000.0
EPS = 1e-6
VMEM_LIMIT = 56 * 1024 * 1024


def _params(n_axes=1, vmem=VMEM_LIMIT):
    return pltpu.CompilerParams(dimension_semantics=("arbitrary",) * n_axes, vmem_limit_bytes=vmem)


def _store_interleaved(ref, a, val):
    rows, n, lanes = ref.shape
    ref.reshape(rows * n, lanes)[pl.ds(a, rows, stride=n), :] = val


def _const_spec(shape):
    zeros = (0,) * len(shape)
    return pl.BlockSpec(shape, lambda *_: zeros, pipeline_mode=pl.Buffered(1))


def _layer_spec(stacked_shape, layer):
    index = (layer,) + (0,) * (len(stacked_shape) - 1)
    return pl.BlockSpec((None,) + tuple(stacked_shape[1:]), lambda *_: index, pipeline_mode=pl.Buffered(1))


def _pack_pair(lo, hi):
    return pltpu.pack_elementwise([lo, hi], packed_dtype=BF16)


def _unpack_pair(words):
    lo = pltpu.unpack_elementwise(words, index=0, packed_dtype=BF16, unpacked_dtype=F32)
    hi = pltpu.unpack_elementwise(words, index=1, packed_dtype=BF16, unpacked_dtype=F32)
    return lo, hi


PROJ_ZR, PROJ_ZI, PROJ_Q, PROJ_K, PROJ_V, PROJ_G = (b * D_HALF for b in range(6))


def _prep_proj_kernel(cs_ref, win_ref, wf_ref, o_ref):
    n = pl.program_id(1)
    hi = lax.Precision.HIGHEST
    groups = [slice(g * LANES, (g + 1) * LANES) for g in range(N_HEADS)]

    @pl.when(n < 2)
    def _():
        cs = cs_ref[pl.ds(pl.multiple_of(n * LANES, LANES), LANES), :]
        for g, cols in enumerate(groups):
            m = jnp.dot(cs, wf_ref[0, g], precision=hi, preferred_element_type=F32)
            o_ref[0, :, cols] = jnp.dot(win_ref[0, :, cols], m, precision=hi, preferred_element_type=F32).astype(BF16)

    @pl.when(n == PROJ_V // D_HALF)
    def _():
        for cols in groups:
            w = win_ref[0, :, cols]
            o_ref[0, :, cols] = (w - jnp.mean(w, axis=-1, keepdims=True)).astype(BF16)

    @pl.when(jnp.logical_and(n >= 2, n != PROJ_V // D_HALF))
    def _():
        o_ref[0] = win_ref[0].astype(BF16)


def _prep_proj_weights(w_in, w_fmix):
    depth, d, _ = w_in.shape
    c = np.arange(LANES)
    ang = 2.0 * np.pi * ((c[:, None] * c[None, :]) % LANES) / LANES
    cs = jnp.asarray(np.concatenate([np.cos(ang), -np.sin(ang)], axis=0), F32)
    n_blocks = 6
    return pl.pallas_call(
        _prep_proj_kernel,
        out_shape=jax.ShapeDtypeStruct((depth, d, n_blocks * D_HALF), BF16),
        grid=(depth, n_blocks),
        in_specs=[
            pl.BlockSpec((2 * LANES, LANES), lambda l, n: (0, 0)),
            pl.BlockSpec((1, d, D_HALF), lambda l, n: (l, 0, jnp.maximum(n - 1, 0))),
            pl.BlockSpec((1, N_HEADS, LANES, LANES), lambda l, n: (l, 0, 0, 0)),
        ],
        out_specs=pl.BlockSpec((1, d, D_HALF), lambda l, n: (l, 0, n)),
        compiler_params=_params(2),
        name="prep_proj_weights",
    )(cs, w_in, w_fmix)


def _rmsnorm(x, gain):
    return x * lax.rsqrt(jnp.mean(x * x, axis=-1, keepdims=True) + EPS) * gain


def _log_sigmoid(x):
    return -(jnp.maximum(-x, 0.0) + jnp.log1p(jnp.exp(-jnp.abs(x))))


def _proj_kernel(x_ref, ln_ref, w_ref, cos_ref, sin_ref, decf_ref, z_ref, q_ref, kt_ref, v_ref, g_ref, sf_ref,
                 carry_ref, *, tiles_per_seq):
    tm = x_ref.shape[0]

    @pl.when(pl.program_id(0) % tiles_per_seq == 0)
    def _():
        carry_ref[...] = jnp.zeros_like(carry_ref)

    h = _rmsnorm(x_ref[...], ln_ref[...]).astype(BF16)

    def project(first_col):
        return jnp.dot(h, w_ref[:, first_col:first_col + D_HALF], preferred_element_type=F32)

    cos = cos_ref[...]
    sin = sin_ref[...]

    def rope(t):
        return t * cos + pltpu.roll(t, LANES // 2, 1) * sin

    kf = project(PROJ_K)
    v = project(PROJ_V).astype(BF16)
    v_ref[...] = v
    qf = project(PROJ_Q)
    k_scale = LANES ** -0.5
    lgf = _log_sigmoid(decf_ref[...])
    lane = lax.broadcasted_iota(jnp.int32, (SUBLANES, LANES), 1).astype(F32)
    for hd in range(N_HEADS):
        cols = slice(hd * LANES, (hd + 1) * LANES)
        q_ref[:, cols] = rope(qf[:, cols]).astype(BF16)
        kt = (rope(kf[:, cols]) * k_scale).T
        k_decay = jnp.exp(lgf[hd] * (float(CHUNK - 1) - lane))[0:1, :]
        c_decay = jnp.exp(lgf[hd] * float(CHUNK))[0:1, :]
        for c in range(tm // CHUNK):
            toks = slice(c * CHUNK, (c + 1) * CHUNK)
            kt_ref[c, cols, :] = kt[:, toks].astype(BF16)
            state = carry_ref[hd]
            sf_ref[c, hd] = state.astype(BF16)
            kv = jnp.dot((kt[:, toks] * k_decay).astype(BF16), v[toks, cols], preferred_element_type=F32)
            carry_ref[hd] = state * c_decay + kv

    zw = _pack_pair(project(PROJ_ZR), project(PROJ_ZI))
    for sl in range(N_HEADS):
        for a in range(tm // LANES):
            _store_interleaved(z_ref.at[sl], a, zw[a * LANES:(a + 1) * LANES, sl * LANES:(sl + 1) * LANES])
    g_ref[...] = project(PROJ_G).astype(BF16)


def _proj(x, ln, w, layer, cos_t, sin_t, dec_f, *, seq):
    t, d = x.shape
    tm = SUBLANES * LANES
    pos_blocks = seq // tm
    n_chunks = tm // CHUNK
    row = lambda i: (i, 0)
    return pl.pallas_call(
        functools.partial(_proj_kernel, tiles_per_seq=pos_blocks),
        out_shape=(
            jax.ShapeDtypeStruct((N_HEADS, LANES, t // LANES, LANES), jnp.uint32),
            jax.ShapeDtypeStruct((t, D_HALF), BF16),
            jax.ShapeDtypeStruct((t // CHUNK, D_HALF, CHUNK), BF16),
            jax.ShapeDtypeStruct((t, D_HALF), BF16),
            jax.ShapeDtypeStruct((t, D_HALF), BF16),
            jax.ShapeDtypeStruct((t // CHUNK, N_HEADS, LANES, LANES), BF16),
        ),
        grid=(t // tm,),
        in_specs=[
            pl.BlockSpec((tm, d), row),
            _const_spec((1, d)),
            _layer_spec(w.shape, layer),
            pl.BlockSpec((tm, LANES), lambda i: (i % pos_blocks, 0)),
            pl.BlockSpec((tm, LANES), lambda i: (i % pos_blocks, 0)),
            _const_spec((N_HEADS, SUBLANES, LANES)),
        ],
        out_specs=(
            pl.BlockSpec((N_HEADS, LANES, tm // LANES, LANES), lambda i: (0, 0, i, 0)),
            pl.BlockSpec((tm, D_HALF), row),
            pl.BlockSpec((n_chunks, D_HALF, CHUNK), lambda i: (i, 0, 0)),
            pl.BlockSpec((tm, D_HALF), row),
            pl.BlockSpec((tm, D_HALF), row),
            pl.BlockSpec((n_chunks, N_HEADS, LANES, LANES), lambda i: (i, 0, 0, 0)),
        ),
        scratch_shapes=[pltpu.VMEM((N_HEADS, LANES, LANES), F32)],
        compiler_params=_params(1),
        name="proj",
    )(x, ln, w, cos_t, sin_t, dec_f)


RET_UNROLL = 2


def _ret_main_kernel(decf_ref, decb_ref, gain_ref, q_ref, kt_ref, v_ref, g_ref, sf_ref, o_ref,
                     mask_ref, qdf_ref, qdb_ref, gs_ref, kv_ref, sb_ref, carry_ref, *, blocks_per_seq):
    n_chunks = kt_ref.shape[0]
    blk = pl.num_programs(0) - 1 - pl.program_id(0)

    @pl.when(blk % blocks_per_seq == blocks_per_seq - 1)
    def _():
        carry_ref[...] = jnp.zeros_like(carry_ref)

    lgb = _log_sigmoid(decb_ref[...])
    lane = lax.broadcasted_iota(jnp.int32, (SUBLANES, LANES), 1).astype(F32)
    k_decay = [jnp.exp(lgb[hd] * lane)[0:1, :] for hd in range(N_HEADS)]
    c_decay = [jnp.exp(lgb[hd] * float(CHUNK))[0:1, :] for hd in range(N_HEADS)]

    def kv_step(i, carry):
        for u in range(RET_UNROLL):
            c = i * RET_UNROLL + u
            rows = pl.ds(pl.multiple_of(c * CHUNK, CHUNK), CHUNK)
            for hd in range(N_HEADS):
                cols = slice(hd * LANES, (hd + 1) * LANES)
                kd = (kt_ref[c, cols, :].astype(F32) * k_decay[hd]).astype(BF16)
                kv_ref[c, hd] = jnp.dot(kd, v_ref[rows, cols], preferred_element_type=F32)
        return carry

    lax.fori_loop(0, n_chunks // RET_UNROLL, kv_step, 0)

    def scan_step(i, carry):
        c = n_chunks - 1 - i
        for hd in range(N_HEADS):
            state = carry_ref[hd]
            sb_ref[c, hd] = state.astype(BF16)
            carry_ref[hd] = state * c_decay[hd] + kv_ref[c, hd]
        return carry

    lax.fori_loop(0, n_chunks, scan_step, 0)

    @pl.when(pl.program_id(0) == 0)
    def _():
        lgf = _log_sigmoid(decf_ref[...])
        lgb = _log_sigmoid(decb_ref[...])
        i = lax.broadcasted_iota(jnp.int32, (CHUNK, CHUNK), 0).astype(F32)
        j = lax.broadcasted_iota(jnp.int32, (CHUNK, CHUNK), 1).astype(F32)
        diff = i - j
        for hd in range(N_HEADS):
            lf = lgf[hd][0:1, :]
            lb = lgb[hd][0:1, :]
            fwd = jnp.where(diff >= 0, jnp.exp(lf * jnp.maximum(diff, 0.0)), 0.0)
            bwd = jnp.where(diff <= 0, jnp.exp(lb * jnp.maximum(-diff, 0.0)), 0.0)
            mask_ref[hd] = fwd + bwd
            qdf_ref[hd] = jnp.exp(lf * (i + 1.0)).astype(BF16)
            qdb_ref[hd] = jnp.exp(lb * (float(CHUNK) - i)).astype(BF16)
        gs_ref[...] = gain_ref[...] * (float(LANES) ** 0.5)

    def one_chunk(c):
        rows = pl.ds(pl.multiple_of(c * CHUNK, CHUNK), CHUNK)
        for hd in range(N_HEADS):
            cols = slice(hd * LANES, (hd + 1) * LANES)
            qh = q_ref[rows, cols]
            scores = jnp.dot(qh, kt_ref[c, cols, :], preferred_element_type=F32) * mask_ref[hd]
            lhs = jnp.concatenate([scores.astype(BF16), qh * qdf_ref[hd], qh * qdb_ref[hd]], axis=1)
            rhs = jnp.concatenate([v_ref[rows, cols], sf_ref[c, hd], sb_ref[c, hd]], axis=0)
            d = jnp.dot(lhs, rhs, preferred_element_type=F32)
            ss = jnp.sum(d * d, axis=-1, keepdims=True)
            yn = d * lax.rsqrt(ss + float(LANES) * EPS) * gs_ref[:, cols]
            gate = g_ref[rows, cols].astype(F32)
            o_ref[rows, cols] = (gate * jax.nn.sigmoid(gate) * yn).astype(BF16)

    def body(i, carry):
        for u in range(RET_UNROLL):
            one_chunk(i * RET_UNROLL + u)
        return carry

    lax.fori_loop(0, n_chunks // RET_UNROLL, body, 0)


def _ret_main(dec_f, dec_b, gain, q, kt, v, g, sf, *, seq, blk):
    t = v.shape[0]
    nb = t // blk
    n_chunks = blk // CHUNK
    assert n_chunks % RET_UNROLL == 0
    row = lambda j: (nb - 1 - j, 0)
    return pl.pallas_call(
        functools.partial(_ret_main_kernel, blocks_per_seq=seq // blk),
        out_shape=jax.ShapeDtypeStruct((t, D_HALF), BF16),
        grid=(nb,),
        in_specs=[
            _const_spec((N_HEADS, SUBLANES, LANES)),
            _const_spec((N_HEADS, SUBLANES, LANES)),
            _const_spec((1, D_HALF)),
            pl.BlockSpec((blk, D_HALF), row),
            pl.BlockSpec((n_chunks, D_HALF, CHUNK), lambda j: (nb - 1 - j, 0, 0)),
            pl.BlockSpec((blk, D_HALF), row),
            pl.BlockSpec((blk, D_HALF), row),
            pl.BlockSpec((n_chunks, N_HEADS, LANES, LANES), lambda j: (nb - 1 - j, 0, 0, 0)),
        ],
        out_specs=pl.BlockSpec((blk, D_HALF), row),
        scratch_shapes=[
            pltpu.VMEM((N_HEADS, CHUNK, CHUNK), F32),
            pltpu.VMEM((N_HEADS, CHUNK, CHUNK), BF16),
            pltpu.VMEM((N_HEADS, CHUNK, CHUNK), BF16),
            pltpu.VMEM((1, D_HALF), F32),
            pltpu.VMEM((n_chunks, N_HEADS, LANES, LANES), F32),
            pltpu.VMEM((n_chunks, N_HEADS, LANES, LANES), BF16),
            pltpu.VMEM((N_HEADS, LANES, LANES), F32),
        ],
        compiler_params=_params(1),
        name="ret_main",
    )(dec_f, dec_b, gain, q, kt, v, g, sf)


def _slab_row(ref, *index):
    return jnp.concatenate([ref[(sl,) + index] for sl in range(N_HEADS)], axis=1)


def _dft_stage_a_kernel(z_ref, g_ref, y_ref):
    r = z_ref.shape[2]
    for jj in range(SUBLANES):
        zr, zi = _unpack_pair(_slab_row(z_ref, jj))
        rhs = jnp.concatenate([zr, zi], axis=0).astype(BF16)
        y = jnp.dot(g_ref[jj], rhs, preferred_element_type=F32)
        yw = _pack_pair(y[:r], y[r:])
        for sl in range(N_HEADS):
            _store_interleaved(y_ref.at[sl], jj, yw[:, sl * LANES:(sl + 1) * LANES])


def _dft_stage_b_kernel(y_ref, h_ref, o_ref):
    for kk in range(SUBLANES):
        yr, yi = _unpack_pair(_slab_row(y_ref, kk))
        rhs = jnp.concatenate([yr, yi], axis=0).astype(BF16)
        out = jnp.dot(h_ref[...], rhs, preferred_element_type=F32)
        for sl in range(N_HEADS):
            _store_interleaved(o_ref.at[sl], kk, out[:, sl * LANES:(sl + 1) * LANES])


def _dft_tables(seq, batch):
    n2 = LANES
    n1 = seq // n2
    idx = np.arange(n1)
    ang_a = 2.0 * np.pi * ((idx[:, None] * idx[None, :]) % n1) / n1
    ang_b = 2.0 * np.pi * (np.arange(n2)[:, None] * idx[None, :]) / seq
    ca, sa = jnp.asarray(np.cos(ang_a), F32)[None], jnp.asarray(np.sin(ang_a), F32)[None]
    cb, sb = jnp.asarray(np.cos(ang_b), F32)[:, :, None], jnp.asarray(np.sin(ang_b), F32)[:, :, None]
    c = ca * cb - sa * sb
    s = sa * cb + ca * sb
    eye = jnp.eye(batch, dtype=F32)[None, :, None, :, None]

    def per_sequence(m):
        return (eye * m[:, None, :, None, :]).reshape(n2, batch * n1, batch * n1)

    g = jnp.concatenate([jnp.concatenate([per_sequence(c), per_sequence(s)], axis=2),
                         jnp.concatenate([per_sequence(-s), per_sequence(c)], axis=2)], axis=1).astype(BF16)
    k2 = np.arange(n2)
    ang2 = 2.0 * np.pi * ((k2[:, None] * k2[None, :]) % n2) / n2
    scale = (seq * LANES) ** -0.5
    hm = jnp.asarray(np.concatenate([np.cos(ang2), np.sin(ang2)], axis=1) * scale, F32).astype(BF16)
    return g, hm


def _dft_two_stage(z, g, hm):
    _, n2, r, _ = z.shape
    y = pl.pallas_call(
        _dft_stage_a_kernel,
        out_shape=jax.ShapeDtypeStruct((N_HEADS, r, n2, LANES), jnp.uint32),
        grid=(n2 // SUBLANES,),
        in_specs=[
            pl.BlockSpec((N_HEADS, SUBLANES, r, LANES), lambda j: (0, j, 0, 0)),
            pl.BlockSpec((SUBLANES, 2 * r, 2 * r), lambda j: (j, 0, 0)),
        ],
        out_specs=pl.BlockSpec((N_HEADS, r, SUBLANES, LANES), lambda j: (0, 0, j, 0)),
        compiler_params=_params(1),
        name="dft_stage_a",
    )(z, g)
    return pl.pallas_call(
        _dft_stage_b_kernel,
        out_shape=jax.ShapeDtypeStruct((N_HEADS, n2, r, LANES), F32),
        grid=(r // SUBLANES,),
        in_specs=[
            pl.BlockSpec((N_HEADS, SUBLANES, n2, LANES), lambda i: (0, i, 0, 0)),
            _const_spec((n2, 2 * n2)),
        ],
        out_specs=pl.BlockSpec((N_HEADS, n2, SUBLANES, LANES), lambda i: (0, 0, i, 0)),
        compiler_params=_params(1),
        name="dft_stage_b",
    )(y, hm)


FF_CHUNK = 1024


def _out_mlp_kernel(x_ref, fo_ref, ro_ref, wo_ref, ln2_ref, w1_ref, w2_ref, lnf_ref, o_ref, *, final):
    tm = x_ref.shape[0]
    fo = jnp.concatenate([fo_ref[sl].reshape(tm, LANES) for sl in range(N_HEADS)], axis=1).astype(BF16)
    x1 = x_ref[...] + jnp.dot(fo, wo_ref[:D_HALF], preferred_element_type=F32)
    x1 = x1 + jnp.dot(ro_ref[...], wo_ref[D_HALF:], preferred_element_type=F32)
    h = _rmsnorm(x1, ln2_ref[...]).astype(BF16)
    acc = None
    for c in range(w1_ref.shape[1] // FF_CHUNK):
        cols = slice(c * FF_CHUNK, (c + 1) * FF_CHUNK)
        a = jnp.dot(h, w1_ref[:, cols], preferred_element_type=F32)
        a = jnp.square(jnp.maximum(a, 0.0)).astype(BF16)
        part = jnp.dot(a, w2_ref[cols, :], preferred_element_type=F32)
        acc = part if acc is None else acc + part
    acc = acc + x1
    if final:
        acc = _rmsnorm(acc, lnf_ref[...])
    o_ref[...] = acc


def _out_mlp(x, fo, ro, wo, ln2, w1, w2, lnf, layer, *, seq, final):
    t, d = x.shape
    tm = 512
    n1 = seq // LANES
    tiles_per_seq = seq // tm
    fo = fo.reshape(N_HEADS, LANES, t // seq, n1, LANES)
    row = lambda i: (i, 0)
    return pl.pallas_call(
        functools.partial(_out_mlp_kernel, final=final),
        out_shape=jax.ShapeDtypeStruct((t, d), F32),
        grid=(t // tm,),
        in_specs=[
            pl.BlockSpec((tm, d), row),
            pl.BlockSpec((N_HEADS, tm // n1, None, n1, LANES), lambda i: (0, i % tiles_per_seq, i // tiles_per_seq, 0, 0)),
            pl.BlockSpec((tm, D_HALF), row),
            _layer_spec(wo.shape, layer),
            _const_spec((1, d)),
            _layer_spec(w1.shape, layer),
            _layer_spec(w2.shape, layer),
            _const_spec((1, d)),
        ],
        out_specs=pl.BlockSpec((tm, d), row),
        compiler_params=_params(1),
        name="out_mlp",
    )(x, fo, ro, wo, ln2, w1, w2, lnf)


def _rope_tables(seq):
    half = LANES // 2
    inv = ROPE_THETA ** (-np.arange(half, dtype=np.float64) / half)
    ang = np.arange(seq, dtype=np.float64)[:, None] * inv[None, :]
    c, s = np.cos(ang), np.sin(ang)
    cos_t = np.concatenate([c, c], axis=1).astype(np.float32)
    sin_t = np.concatenate([-s, s], axis=1).astype(np.float32)
    return jnp.asarray(cos_t), jnp.asarray(sin_t)


def _bcast_heads(v):
    return jnp.broadcast_to(v.astype(F32)[:, None, None], (N_HEADS, SUBLANES, LANES))


def _trunk(x, weights, *, seq):
    wproj, wo, w1, w2, ln1, ln2, dec_f, dec_b, gain, lnf = weights
    depth = wproj.shape[0]
    cos_t, sin_t = _rope_tables(seq)
    g_dft, h_dft = _dft_tables(seq, x.shape[0] // seq)
    blk = min(seq, 2048)
    for l in range(depth):
        df, db = _bcast_heads(dec_f[l]), _bcast_heads(dec_b[l])
        z, q, kt, v, g, sf = _proj(x, ln1[l][None], wproj, l, cos_t, sin_t, df, seq=seq)
        ro = _ret_main(df, db, gain[l][None].astype(F32), q, kt, v, g, sf, seq=seq, blk=blk)
        fo = _dft_two_stage(z, g_dft, h_dft)
        x = _out_mlp(x, fo, ro, wo, ln2[l][None], w1, w2, lnf[None], l, seq=seq, final=(l == depth - 1))
    return x


def kernel(x_prompt, x_sample, ln1, w_in, w_fmix, decay_fwd, decay_bwd, gn_gain, w_o, ln2, w_ff1, w_ff2, ln_final):
    d = x_prompt.shape[-1]
    weights = (_prep_proj_weights(w_in, w_fmix), w_o.astype(BF16), w_ff1.astype(BF16), w_ff2.astype(BF16),
               ln1, ln2, decay_fwd, decay_bwd, gn_gain, ln_final)
    y_prompt = _trunk(x_prompt.reshape(-1, d), weights, seq=x_prompt.shape[1])
    y_sample = _trunk(x_sample.reshape(-1, d), weights, seq=x_sample.shape[1])
    return y_prompt.reshape(x_prompt.shape), y_sample.reshape(x_sample.shape)
```

```python
import functools

import jax
import jax.numpy as jnp
import numpy as np
from jax import lax
from jax.experimental import pallas as pl
from jax.experimental.pallas import tpu as pltpu

F32 = jnp.float32
BF16 = jnp.bfloat16

LANES = 128
SUBLANES = 8
N_HEADS = 4
D_HALF = N_HEADS * LANES
CHUNK = LANES
U_SLABS = D_HALF // (2 * LANES)
ROPE_THETA = 10000.0
EPS = 1e-6
VMEM_LIMIT = 56 * 1024 * 1024


def _params(n_axes=1, vmem=VMEM_LIMIT):
    return pltpu.CompilerParams(dimension_semantics=("arbitrary",) * n_axes, vmem_limit_bytes=vmem)


def _store_interleaved(ref, a, val):
    rows, n, lanes = ref.shape
    ref.reshape(rows * n, lanes)[pl.ds(a, rows, stride=n), :] = val


def _const_spec(shape):
    zeros = (0,) * len(shape)
    return pl.BlockSpec(shape, lambda *_: zeros, pipeline_mode=pl.Buffered(1))


def _layer_spec(stacked_shape, layer):
    index = (layer,) + (0,) * (len(stacked_shape) - 1)
    return pl.BlockSpec((None,) + tuple(stacked_shape[1:]), lambda *_: index, pipeline_mode=pl.Buffered(1))


def _pack_pair(lo, hi):
    return pltpu.pack_elementwise([lo, hi], packed_dtype=BF16)


def _unpack_pair(words):
    lo = pltpu.unpack_elementwise(words, index=0, packed_dtype=BF16, unpacked_dtype=F32)
    hi = pltpu.unpack_elementwise(words, index=1, packed_dtype=BF16, unpacked_dtype=F32)
    return lo, hi


PROJ_U, PROJ_Q, PROJ_K, PROJ_V, PROJ_G = (b * D_HALF for b in range(5))


def _prep_proj_kernel(cs_ref, win_ref, wf_ref, o_ref, m_ref):
    n = pl.program_id(1)

    @pl.when(n == 0)
    def _():
        for g in range(N_HEADS):
            cw = jnp.dot(cs_ref[...], wf_ref[0, g], precision=lax.Precision.HIGHEST, preferred_element_type=F32)
            m_ref[0, g] = jnp.concatenate([cw[:LANES], cw[LANES:]], axis=1).astype(BF16)

    @pl.when(n == PROJ_V // D_HALF)
    def _():
        for g in range(N_HEADS):
            cols = slice(g * LANES, (g + 1) * LANES)
            w = win_ref[0, :, cols]
            o_ref[0, :, cols] = (w - jnp.mean(w, axis=-1, keepdims=True)).astype(BF16)

    @pl.when(n != PROJ_V // D_HALF)
    def _():
        o_ref[0] = win_ref[0].astype(BF16)


def _prep_proj_weights(w_in, w_fmix):
    depth, d, width = w_in.shape
    c = np.arange(LANES)
    ang = 2.0 * np.pi * ((c[:, None] * c[None, :]) % LANES) / LANES
    cs = jnp.asarray(np.concatenate([np.cos(ang), -np.sin(ang)], axis=0), F32)
    return pl.pallas_call(
        _prep_proj_kernel,
        out_shape=(
            jax.ShapeDtypeStruct((depth, d, width), BF16),
            jax.ShapeDtypeStruct((depth, N_HEADS, LANES, 2 * LANES), BF16),
        ),
        grid=(depth, width // D_HALF),
        in_specs=[
            pl.BlockSpec((2 * LANES, LANES), lambda l, n: (0, 0)),
            pl.BlockSpec((1, d, D_HALF), lambda l, n: (l, 0, n)),
            pl.BlockSpec((1, N_HEADS, LANES, LANES), lambda l, n: (l, 0, 0, 0)),
        ],
        out_specs=(
            pl.BlockSpec((1, d, D_HALF), lambda l, n: (l, 0, n)),
            pl.BlockSpec((1, N_HEADS, LANES, 2 * LANES), lambda l, n: (l, 0, 0, 0)),
        ),
        compiler_params=_params(2),
        name="prep_proj_weights",
    )(cs, w_in, w_fmix)


def _rmsnorm(x, gain):
    return x * lax.rsqrt(jnp.mean(x * x, axis=-1, keepdims=True) + EPS) * gain


def _log_sigmoid(x):
    return -(jnp.maximum(-x, 0.0) + jnp.log1p(jnp.exp(-jnp.abs(x))))


def _proj_kernel(x_ref, ln_ref, w_ref, cos_ref, sin_ref, decf_ref, u_ref, q_ref, kt_ref, v_ref, g_ref, sf_ref,
                 carry_ref, *, tiles_per_seq):
    tm = x_ref.shape[0]

    @pl.when(pl.program_id(0) % tiles_per_seq == 0)
    def _():
        carry_ref[...] = jnp.zeros_like(carry_ref)

    h = _rmsnorm(x_ref[...], ln_ref[...]).astype(BF16)

    def project(first_col):
        return jnp.dot(h, w_ref[:, first_col:first_col + D_HALF], preferred_element_type=F32)

    cos = cos_ref[...]
    sin = sin_ref[...]

    def rope(t):
        return t * cos + pltpu.roll(t, LANES // 2, 1) * sin

    kf = project(PROJ_K)
    v = project(PROJ_V).astype(BF16)
    v_ref[...] = v
    qf = project(PROJ_Q)
    k_scale = LANES ** -0.5
    lgf = _log_sigmoid(decf_ref[...])
    lane = lax.broadcasted_iota(jnp.int32, (SUBLANES, LANES), 1).astype(F32)
    for hd in range(N_HEADS):
        cols = slice(hd * LANES, (hd + 1) * LANES)
        q_ref[:, cols] = rope(qf[:, cols]).astype(BF16)
        kt = (rope(kf[:, cols]) * k_scale).T
        k_decay = jnp.exp(lgf[hd] * (float(CHUNK - 1) - lane))[0:1, :]
        c_decay = jnp.exp(lgf[hd] * float(CHUNK))[0:1, :]
        for c in range(tm // CHUNK):
            toks = slice(c * CHUNK, (c + 1) * CHUNK)
            kt_ref[c, cols, :] = kt[:, toks].astype(BF16)
            state = carry_ref[hd]
            sf_ref[c, hd] = state.astype(BF16)
            kv = jnp.dot((kt[:, toks] * k_decay).astype(BF16), v[toks, cols], preferred_element_type=F32)
            carry_ref[hd] = state * c_decay + kv

    u = project(PROJ_U)
    uw = _pack_pair(u[:, :D_HALF // 2], u[:, D_HALF // 2:])
    for sl in range(u_ref.shape[0]):
        for a in range(tm // LANES):
            _store_interleaved(u_ref.at[sl], a, uw[a * LANES:(a + 1) * LANES, sl * LANES:(sl + 1) * LANES])
    g_ref[...] = project(PROJ_G).astype(BF16)


def _proj(x, ln, w, layer, cos_t, sin_t, dec_f, *, seq):
    t, d = x.shape
    tm = SUBLANES * LANES
    pos_blocks = seq // tm
    n_chunks = tm // CHUNK
    row = lambda i: (i, 0)
    return pl.pallas_call(
        functools.partial(_proj_kernel, tiles_per_seq=pos_blocks),
        out_shape=(
            jax.ShapeDtypeStruct((U_SLABS, LANES, t // LANES, LANES), jnp.uint32),
            jax.ShapeDtypeStruct((t, D_HALF), BF16),
            jax.ShapeDtypeStruct((t // CHUNK, D_HALF, CHUNK), BF16),
            jax.ShapeDtypeStruct((t, D_HALF), BF16),
            jax.ShapeDtypeStruct((t, D_HALF), BF16),
            jax.ShapeDtypeStruct((t // CHUNK, N_HEADS, LANES, LANES), BF16),
        ),
        grid=(t // tm,),
        in_specs=[
            pl.BlockSpec((tm, d), row),
            _const_spec((1, d)),
            _layer_spec(w.shape, layer),
            pl.BlockSpec((tm, LANES), lambda i: (i % pos_blocks, 0)),
            pl.BlockSpec((tm, LANES), lambda i: (i % pos_blocks, 0)),
            _const_spec((N_HEADS, SUBLANES, LANES)),
        ],
        out_specs=(
            pl.BlockSpec((U_SLABS, LANES, tm // LANES, LANES), lambda i: (0, 0, i, 0)),
            pl.BlockSpec((tm, D_HALF), row),
            pl.BlockSpec((n_chunks, D_HALF, CHUNK), lambda i: (i, 0, 0)),
            pl.BlockSpec((tm, D_HALF), row),
            pl.BlockSpec((tm, D_HALF), row),
            pl.BlockSpec((n_chunks, N_HEADS, LANES, LANES), lambda i: (i, 0, 0, 0)),
        ),
        scratch_shapes=[pltpu.VMEM((N_HEADS, LANES, LANES), F32)],
        compiler_params=_params(1),
        name="proj",
    )(x, ln, w, cos_t, sin_t, dec_f)


RET_UNROLL = 2


def _ret_main_kernel(decf_ref, decb_ref, gain_ref, q_ref, kt_ref, v_ref, g_ref, sf_ref, o_ref,
                     mask_ref, qdf_ref, qdb_ref, gs_ref, kv_ref, sb_ref, carry_ref, *, blocks_per_seq):
    n_chunks = kt_ref.shape[0]
    blk = pl.num_programs(0) - 1 - pl.program_id(0)

    @pl.when(blk % blocks_per_seq == blocks_per_seq - 1)
    def _():
        carry_ref[...] = jnp.zeros_like(carry_ref)

    lgb = _log_sigmoid(decb_ref[...])
    lane = lax.broadcasted_iota(jnp.int32, (SUBLANES, LANES), 1).astype(F32)
    k_decay = [jnp.exp(lgb[hd] * lane)[0:1, :] for hd in range(N_HEADS)]
    c_decay = [jnp.exp(lgb[hd] * float(CHUNK))[0:1, :] for hd in range(N_HEADS)]

    def kv_step(i, carry):
        for u in range(RET_UNROLL):
            c = i * RET_UNROLL + u
            rows = pl.ds(pl.multiple_of(c * CHUNK, CHUNK), CHUNK)
            for hd in range(N_HEADS):
                cols = slice(hd * LANES, (hd + 1) * LANES)
                kd = (kt_ref[c, cols, :].astype(F32) * k_decay[hd]).astype(BF16)
                kv_ref[c, hd] = jnp.dot(kd, v_ref[rows, cols], preferred_element_type=F32)
        return carry

    lax.fori_loop(0, n_chunks // RET_UNROLL, kv_step, 0)

    def scan_step(i, carry):
        c = n_chunks - 1 - i
        for hd in range(N_HEADS):
            state = carry_ref[hd]
            sb_ref[c, hd] = state.astype(BF16)
            carry_ref[hd] = state * c_decay[hd] + kv_ref[c, hd]
        return carry

    lax.fori_loop(0, n_chunks, scan_step, 0)

    @pl.when(pl.program_id(0) == 0)
    def _():
        lgf = _log_sigmoid(decf_ref[...])
        lgb = _log_sigmoid(decb_ref[...])
        i = lax.broadcasted_iota(jnp.int32, (CHUNK, CHUNK), 0).astype(F32)
        j = lax.broadcasted_iota(jnp.int32, (CHUNK, CHUNK), 1).astype(F32)
        diff = i - j
        for hd in range(N_HEADS):
            lf = lgf[hd][0:1, :]
            lb = lgb[hd][0:1, :]
            fwd = jnp.where(diff >= 0, jnp.exp(lf * jnp.maximum(diff, 0.0)), 0.0)
            bwd = jnp.where(diff <= 0, jnp.exp(lb * jnp.maximum(-diff, 0.0)), 0.0)
            mask_ref[hd] = fwd + bwd
            qdf_ref[hd] = jnp.exp(lf * (i + 1.0)).astype(BF16)
            qdb_ref[hd] = jnp.exp(lb * (float(CHUNK) - i)).astype(BF16)
        gs_ref[...] = gain_ref[...] * (float(LANES) ** 0.5)

    def one_chunk(c):
        rows = pl.ds(pl.multiple_of(c * CHUNK, CHUNK), CHUNK)
        for hd in range(N_HEADS):
            cols = slice(hd * LANES, (hd + 1) * LANES)
            qh = q_ref[rows, cols]
            scores = jnp.dot(qh, kt_ref[c, cols, :], preferred_element_type=F32) * mask_ref[hd]
            lhs = jnp.concatenate([scores.astype(BF16), qh * qdf_ref[hd], qh * qdb_ref[hd]], axis=1)
            rhs = jnp.concatenate([v_ref[rows, cols], sf_ref[c, hd], sb_ref[c, hd]], axis=0)
            d = jnp.dot(lhs, rhs, preferred_element_type=F32)
            ss = jnp.sum(d * d, axis=-1, keepdims=True)
            yn = d * lax.rsqrt(ss + float(LANES) * EPS) * gs_ref[:, cols]
            gate = g_ref[rows, cols].astype(F32)
            o_ref[rows, cols] = (gate * jax.nn.sigmoid(gate) * yn).astype(BF16)

    def body(i, carry):
        for u in range(RET_UNROLL):
            one_chunk(i * RET_UNROLL + u)
        return carry

    lax.fori_loop(0, n_chunks // RET_UNROLL, body, 0)


def _ret_main(dec_f, dec_b, gain, q, kt, v, g, sf, *, seq, blk):
    t = v.shape[0]
    nb = t // blk
    n_chunks = blk // CHUNK
    assert n_chunks % RET_UNROLL == 0
    row = lambda j: (nb - 1 - j, 0)
    return pl.pallas_call(
        functools.partial(_ret_main_kernel, blocks_per_seq=seq // blk),
        out_shape=jax.ShapeDtypeStruct((t, D_HALF), BF16),
        grid=(nb,),
        in_specs=[
            _const_spec((N_HEADS, SUBLANES, LANES)),
            _const_spec((N_HEADS, SUBLANES, LANES)),
            _const_spec((1, D_HALF)),
            pl.BlockSpec((blk, D_HALF), row),
            pl.BlockSpec((n_chunks, D_HALF, CHUNK), lambda j: (nb - 1 - j, 0, 0)),
            pl.BlockSpec((blk, D_HALF), row),
            pl.BlockSpec((blk, D_HALF), row),
            pl.BlockSpec((n_chunks, N_HEADS, LANES, LANES), lambda j: (nb - 1 - j, 0, 0, 0)),
        ],
        out_specs=pl.BlockSpec((blk, D_HALF), row),
        scratch_shapes=[
            pltpu.VMEM((N_HEADS, CHUNK, CHUNK), F32),
            pltpu.VMEM((N_HEADS, CHUNK, CHUNK), BF16),
            pltpu.VMEM((N_HEADS, CHUNK, CHUNK), BF16),
            pltpu.VMEM((1, D_HALF), F32),
            pltpu.VMEM((n_chunks, N_HEADS, LANES, LANES), F32),
            pltpu.VMEM((n_chunks, N_HEADS, LANES, LANES), BF16),
            pltpu.VMEM((N_HEADS, LANES, LANES), F32),
        ],
        compiler_params=_params(1),
        name="ret_main",
    )(dec_f, dec_b, gain, q, kt, v, g, sf)


def _slab_row(ref, *index):
    return jnp.concatenate([ref[(sl,) + index] for sl in range(ref.shape[0])], axis=1)


def _dft_stage_a_kernel(u_ref, m_ref, g_ref, y_ref):
    n_s2, r = u_ref.shape[1:3]
    words = jnp.concatenate([u_ref[sl].reshape(n_s2 * r, LANES) for sl in range(U_SLABS)], axis=1)
    lo, hi = _unpack_pair(words)
    u = jnp.concatenate([lo, hi], axis=1).astype(BF16)
    zz = [jnp.dot(u[:, g * LANES:(g + 1) * LANES], m_ref[g], preferred_element_type=F32) for g in range(N_HEADS)]
    zr = jnp.concatenate([z[:, :LANES] for z in zz], axis=1).astype(BF16)
    zi = jnp.concatenate([z[:, LANES:] for z in zz], axis=1).astype(BF16)
    for jj in range(n_s2):
        rows = slice(jj * r, (jj + 1) * r)
        rhs = jnp.concatenate([zr[rows], zi[rows]], axis=0)
        y = jnp.dot(g_ref[jj], rhs, preferred_element_type=F32)
        yw = _pack_pair(y[:r], y[r:])
        for sl in range(N_HEADS):
            _store_interleaved(y_ref.at[sl], jj, yw[:, sl * LANES:(sl + 1) * LANES])


def _dft_stage_b_kernel(y_ref, h_ref, o_ref):
    for kk in range(SUBLANES):
        yr, yi = _unpack_pair(_slab_row(y_ref, kk))
        rhs = jnp.concatenate([yr, yi], axis=0).astype(BF16)
        out = jnp.dot(h_ref[...], rhs, preferred_element_type=F32)
        for sl in range(N_HEADS):
            _store_interleaved(o_ref.at[sl], kk, out[:, sl * LANES:(sl + 1) * LANES])


def _dft_tables(seq, batch):
    n2 = LANES
    n1 = seq // n2
    idx = np.arange(n1)
    ang_a = 2.0 * np.pi * ((idx[:, None] * idx[None, :]) % n1) / n1
    ang_b = 2.0 * np.pi * (np.arange(n2)[:, None] * idx[None, :]) / seq
    ca, sa = jnp.asarray(np.cos(ang_a), F32)[None], jnp.asarray(np.sin(ang_a), F32)[None]
    cb, sb = jnp.asarray(np.cos(ang_b), F32)[:, :, None], jnp.asarray(np.sin(ang_b), F32)[:, :, None]
    c = ca * cb - sa * sb
    s = sa * cb + ca * sb
    eye = jnp.eye(batch, dtype=F32)[None, :, None, :, None]

    def per_sequence(m):
        return (eye * m[:, None, :, None, :]).reshape(n2, batch * n1, batch * n1)

    g = jnp.concatenate([jnp.concatenate([per_sequence(c), per_sequence(s)], axis=2),
                         jnp.concatenate([per_sequence(-s), per_sequence(c)], axis=2)], axis=1).astype(BF16)
    k2 = np.arange(n2)
    ang2 = 2.0 * np.pi * ((k2[:, None] * k2[None, :]) % n2) / n2
    scale = (seq * LANES) ** -0.5
    hm = jnp.asarray(np.concatenate([np.cos(ang2), np.sin(ang2)], axis=1) * scale, F32).astype(BF16)
    return g, hm


def _dft_two_stage(u, m, layer, g, hm):
    _, n2, r, _ = u.shape
    y = pl.pallas_call(
        _dft_stage_a_kernel,
        out_shape=jax.ShapeDtypeStruct((N_HEADS, r, n2, LANES), jnp.uint32),
        grid=(n2 // SUBLANES,),
        in_specs=[
            pl.BlockSpec((U_SLABS, SUBLANES, r, LANES), lambda j: (0, j, 0, 0)),
            _layer_spec(m.shape, layer),
            pl.BlockSpec((SUBLANES, 2 * r, 2 * r), lambda j: (j, 0, 0)),
        ],
        out_specs=pl.BlockSpec((N_HEADS, r, SUBLANES, LANES), lambda j: (0, 0, j, 0)),
        compiler_params=_params(1),
        name="dft_stage_a",
    )(u, m, g)
    return pl.pallas_call(
        _dft_stage_b_kernel,
        out_shape=jax.ShapeDtypeStruct((N_HEADS, n2, r, LANES), F32),
        grid=(r // SUBLANES,),
        in_specs=[
            pl.BlockSpec((N_HEADS, SUBLANES, n2, LANES), lambda i: (0, i, 0, 0)),
            _const_spec((n2, 2 * n2)),
        ],
        out_specs=pl.BlockSpec((N_HEADS, n2, SUBLANES, LANES), lambda i: (0, 0, i, 0)),
        compiler_params=_params(1),
        name="dft_stage_b",
    )(y, hm)


FF_CHUNK = 1024


def _out_mlp_kernel(x_ref, fo_ref, ro_ref, wo_ref, ln2_ref, w1_ref, w2_ref, lnf_ref, o_ref, *, final):
    tm = x_ref.shape[0]
    fo = jnp.concatenate([fo_ref[sl].reshape(tm, LANES) for sl in range(N_HEADS)], axis=1).astype(BF16)
    x1 = x_ref[...] + jnp.dot(fo, wo_ref[:D_HALF], preferred_element_type=F32)
    x1 = x1 + jnp.dot(ro_ref[...], wo_ref[D_HALF:], preferred_element_type=F32)
    h = _rmsnorm(x1, ln2_ref[...]).astype(BF16)
    acc = None
    for c in range(w1_ref.shape[1] // FF_CHUNK):
        cols = slice(c * FF_CHUNK, (c + 1) * FF_CHUNK)
        a = jnp.dot(h, w1_ref[:, cols], preferred_element_type=F32)
        a = jnp.square(jnp.maximum(a, 0.0)).astype(BF16)
        part = jnp.dot(a, w2_ref[cols, :], preferred_element_type=F32)
        acc = part if acc is None else acc + part
    acc = acc + x1
    if final:
        acc = _rmsnorm(acc, lnf_ref[...])
    o_ref[...] = acc


def _out_mlp(x, fo, ro, wo, ln2, w1, w2, lnf, layer, *, seq, final):
    t, d = x.shape
    tm = 512
    n1 = seq // LANES
    tiles_per_seq = seq // tm
    fo = fo.reshape(N_HEADS, LANES, t // seq, n1, LANES)
    row = lambda i: (i, 0)
    return pl.pallas_call(
        functools.partial(_out_mlp_kernel, final=final),
        out_shape=jax.ShapeDtypeStruct((t, d), F32),
        grid=(t // tm,),
        in_specs=[
            pl.BlockSpec((tm, d), row),
            pl.BlockSpec((N_HEADS, tm // n1, None, n1, LANES), lambda i: (0, i % tiles_per_seq, i // tiles_per_seq, 0, 0)),
            pl.BlockSpec((tm, D_HALF), row),
            _layer_spec(wo.shape, layer),
            _const_spec((1, d)),
            _layer_spec(w1.shape, layer),
            _layer_spec(w2.shape, layer),
            _const_spec((1, d)),
        ],
        out_specs=pl.BlockSpec((tm, d), row),
        compiler_params=_params(1),
        name="out_mlp",
    )(x, fo, ro, wo, ln2, w1, w2, lnf)


def _rope_tables(seq):
    half = LANES // 2
    inv = ROPE_THETA ** (-np.arange(half, dtype=np.float64) / half)
    ang = np.arange(seq, dtype=np.float64)[:, None] * inv[None, :]
    c, s = np.cos(ang), np.sin(ang)
    cos_t = np.concatenate([c, c], axis=1).astype(np.float32)
    sin_t = np.concatenate([-s, s], axis=1).astype(np.float32)
    return jnp.asarray(cos_t), jnp.asarray(sin_t)


def _bcast_heads(v):
    return jnp.broadcast_to(v.astype(F32)[:, None, None], (N_HEADS, SUBLANES, LANES))


def _trunk(x, weights, *, seq):
    (wproj, m_dft), wo, w1, w2, ln1, ln2, dec_f, dec_b, gain, lnf = weights
    depth = wproj.shape[0]
    cos_t, sin_t = _rope_tables(seq)
    g_dft, h_dft = _dft_tables(seq, x.shape[0] // seq)
    blk = min(seq, 2048)
    for l in range(depth):
        df, db = _bcast_heads(dec_f[l]), _bcast_heads(dec_b[l])
        u, q, kt, v, g, sf = _proj(x, ln1[l][None], wproj, l, cos_t, sin_t, df, seq=seq)
        ro = _ret_main(df, db, gain[l][None].astype(F32), q, kt, v, g, sf, seq=seq, blk=blk)
        fo = _dft_two_stage(u, m_dft, l, g_dft, h_dft)
        x = _out_mlp(x, fo, ro, wo, ln2[l][None], w1, w2, lnf[None], l, seq=seq, final=(l == depth - 1))
    return x


def kernel(x_prompt, x_sample, ln1, w_in, w_fmix, decay_fwd, decay_bwd, gn_gain, w_o, ln2, w_ff1, w_ff2, ln_final):
    d = x_prompt.shape[-1]
    weights = (_prep_proj_weights(w_in, w_fmix), w_o.astype(BF16), w_ff1.astype(BF16), w_ff2.astype(BF16),
               ln1, ln2, decay_fwd, decay_bwd, gn_gain, ln_final)
    y_prompt = _trunk(x_prompt.reshape(-1, d), weights, seq=x_prompt.shape[1])
    y_sample = _trunk(x_sample.reshape(-1, d), weights, seq=x_sample.shape[1])
    return y_prompt.reshape(x_prompt.shape), y_sample.reshape(x_sample.shape)
```

```python
import functools

import jax
import jax.numpy as jnp
import numpy as np
from jax import lax
from jax.experimental import pallas as pl
from jax.experimental.pallas import tpu as pltpu

F32 = jnp.float32
BF16 = jnp.bfloat16

LANES = 128
SUBLANES = 8
N_HEADS = 4
D_HALF = N_HEADS * LANES
CHUNK = LANES
U_SLABS = D_HALF // (2 * LANES)
ROPE_THETA = 10000.0
EPS = 1e-6
VMEM_LIMIT = 60 * 1024 * 1024
DFT_ROWS = 16


def _params(n_axes=1, vmem=VMEM_LIMIT):
    return pltpu.CompilerParams(dimension_semantics=("arbitrary",) * n_axes, vmem_limit_bytes=vmem)


def _store_interleaved(ref, a, val):
    rows, n, lanes = ref.shape
    ref.reshape(rows * n, lanes)[pl.ds(a, rows, stride=n), :] = val


def _const_spec(shape):
    zeros = (0,) * len(shape)
    return pl.BlockSpec(shape, lambda *_: zeros, pipeline_mode=pl.Buffered(1))


def _layer_spec(stacked_shape, layer):
    index = (layer,) + (0,) * (len(stacked_shape) - 1)
    return pl.BlockSpec((None,) + tuple(stacked_shape[1:]), lambda *_: index, pipeline_mode=pl.Buffered(1))


def _pack_pair(lo, hi):
    return pltpu.pack_elementwise([lo, hi], packed_dtype=BF16)


def _unpack_pair(words):
    lo = pltpu.unpack_elementwise(words, index=0, packed_dtype=BF16, unpacked_dtype=F32)
    hi = pltpu.unpack_elementwise(words, index=1, packed_dtype=BF16, unpacked_dtype=F32)
    return lo, hi


PROJ_U, PROJ_Q, PROJ_K, PROJ_V, PROJ_G = (b * D_HALF for b in range(5))


def _prep_proj_kernel(cs_ref, win_ref, wf_ref, o_ref, m_ref):
    n = pl.program_id(1)

    @pl.when(n == 0)
    def _():
        for g in range(N_HEADS):
            cw = jnp.dot(cs_ref[...], wf_ref[0, g], precision=lax.Precision.HIGHEST, preferred_element_type=F32)
            m_ref[0, g] = jnp.concatenate([cw[:LANES], cw[LANES:]], axis=1).astype(BF16)

    @pl.when(n == PROJ_V // D_HALF)
    def _():
        for g in range(N_HEADS):
            cols = slice(g * LANES, (g + 1) * LANES)
            w = win_ref[0, :, cols]
            o_ref[0, :, cols] = (w - jnp.mean(w, axis=-1, keepdims=True)).astype(BF16)

    @pl.when(n != PROJ_V // D_HALF)
    def _():
        o_ref[0] = win_ref[0].astype(BF16)


def _prep_proj_weights(w_in, w_fmix):
    depth, d, width = w_in.shape
    c = np.arange(LANES)
    ang = 2.0 * np.pi * ((c[:, None] * c[None, :]) % LANES) / LANES
    cs = jnp.asarray(np.concatenate([np.cos(ang), -np.sin(ang)], axis=0), F32)
    return pl.pallas_call(
        _prep_proj_kernel,
        out_shape=(
            jax.ShapeDtypeStruct((depth, d, width), BF16),
            jax.ShapeDtypeStruct((depth, N_HEADS, LANES, 2 * LANES), BF16),
        ),
        grid=(depth, width // D_HALF),
        in_specs=[
            pl.BlockSpec((2 * LANES, LANES), lambda l, n: (0, 0)),
            pl.BlockSpec((1, d, D_HALF), lambda l, n: (l, 0, n)),
            pl.BlockSpec((1, N_HEADS, LANES, LANES), lambda l, n: (l, 0, 0, 0)),
        ],
        out_specs=(
            pl.BlockSpec((1, d, D_HALF), lambda l, n: (l, 0, n)),
            pl.BlockSpec((1, N_HEADS, LANES, 2 * LANES), lambda l, n: (l, 0, 0, 0)),
        ),
        compiler_params=_params(2),
        name="prep_proj_weights",
    )(cs, w_in, w_fmix)


def _rmsnorm(x, gain):
    return x * lax.rsqrt(jnp.mean(x * x, axis=-1, keepdims=True) + EPS) * gain


def _log_sigmoid(x):
    return -(jnp.maximum(-x, 0.0) + jnp.log1p(jnp.exp(-jnp.abs(x))))


def _proj_kernel(x_ref, ln_ref, w_ref, cos_ref, sin_ref, decf_ref, u_ref, q_ref, kt_ref, v_ref, g_ref, sf_ref,
                 carry_ref, *, tiles_per_seq):
    tm = x_ref.shape[0]

    @pl.when(pl.program_id(0) % tiles_per_seq == 0)
    def _():
        carry_ref[...] = jnp.zeros_like(carry_ref)

    h = _rmsnorm(x_ref[...], ln_ref[...]).astype(BF16)

    def project(first_col):
        return jnp.dot(h, w_ref[:, first_col:first_col + D_HALF], preferred_element_type=F32)

    cos = cos_ref[...]
    sin = sin_ref[...]

    def rope(t):
        return t * cos + pltpu.roll(t, LANES // 2, 1) * sin

    kf = project(PROJ_K)
    v = project(PROJ_V).astype(BF16)
    v_ref[...] = v
    qf = project(PROJ_Q)
    k_scale = LANES ** -0.5
    lgf = _log_sigmoid(decf_ref[...])
    lane = lax.broadcasted_iota(jnp.int32, (SUBLANES, LANES), 1).astype(F32)
    for hd in range(N_HEADS):
        cols = slice(hd * LANES, (hd + 1) * LANES)
        q_ref[:, cols] = rope(qf[:, cols]).astype(BF16)
        kt = (rope(kf[:, cols]) * k_scale).T
        k_decay = jnp.exp(lgf[hd] * (float(CHUNK - 1) - lane))[0:1, :]
        c_decay = jnp.exp(lgf[hd] * float(CHUNK))[0:1, :]
        for c in range(tm // CHUNK):
            toks = slice(c * CHUNK, (c + 1) * CHUNK)
            kt_ref[c, cols, :] = kt[:, toks].astype(BF16)
            state = carry_ref[hd]
            sf_ref[c, hd] = state.astype(BF16)
            kv = jnp.dot((kt[:, toks] * k_decay).astype(BF16), v[toks, cols], preferred_element_type=F32)
            carry_ref[hd] = state * c_decay + kv

    u = project(PROJ_U)
    uw = _pack_pair(u[:, :D_HALF // 2], u[:, D_HALF // 2:])
    for sl in range(u_ref.shape[0]):
        for a in range(tm // LANES):
            _store_interleaved(u_ref.at[sl], a, uw[a * LANES:(a + 1) * LANES, sl * LANES:(sl + 1) * LANES])
    g_ref[...] = project(PROJ_G).astype(BF16)


def _proj(x, ln, w, layer, cos_t, sin_t, dec_f, *, seq):
    t, d = x.shape
    tm = SUBLANES * LANES
    pos_blocks = seq // tm
    n_chunks = tm // CHUNK
    row = lambda i: (i, 0)
    return pl.pallas_call(
        functools.partial(_proj_kernel, tiles_per_seq=pos_blocks),
        out_shape=(
            jax.ShapeDtypeStruct((U_SLABS, LANES, t // LANES, LANES), jnp.uint32),
            jax.ShapeDtypeStruct((t, D_HALF), BF16),
            jax.ShapeDtypeStruct((t // CHUNK, D_HALF, CHUNK), BF16),
            jax.ShapeDtypeStruct((t, D_HALF), BF16),
            jax.ShapeDtypeStruct((t, D_HALF), BF16),
            jax.ShapeDtypeStruct((t // CHUNK, N_HEADS, LANES, LANES), BF16),
        ),
        grid=(t // tm,),
        in_specs=[
            pl.BlockSpec((tm, d), row),
            _const_spec((1, d)),
            _layer_spec(w.shape, layer),
            pl.BlockSpec((tm, LANES), lambda i: (i % pos_blocks, 0)),
            pl.BlockSpec((tm, LANES), lambda i: (i % pos_blocks, 0)),
            _const_spec((N_HEADS, SUBLANES, LANES)),
        ],
        out_specs=(
            pl.BlockSpec((U_SLABS, LANES, tm // LANES, LANES), lambda i: (0, 0, i, 0)),
            pl.BlockSpec((tm, D_HALF), row),
            pl.BlockSpec((n_chunks, D_HALF, CHUNK), lambda i: (i, 0, 0)),
            pl.BlockSpec((tm, D_HALF), row),
            pl.BlockSpec((tm, D_HALF), row),
            pl.BlockSpec((n_chunks, N_HEADS, LANES, LANES), lambda i: (i, 0, 0, 0)),
        ),
        scratch_shapes=[pltpu.VMEM((N_HEADS, LANES, LANES), F32)],
        compiler_params=_params(1),
        name="proj",
    )(x, ln, w, cos_t, sin_t, dec_f)


RET_UNROLL = 4


def _ret_main_kernel(decf_ref, decb_ref, gain_ref, q_ref, kt_ref, v_ref, g_ref, sf_ref, o_ref,
                     mask_ref, qdf_ref, qdb_ref, gs_ref, kv_ref, sb_ref, carry_ref, *, blocks_per_seq):
    n_chunks = kt_ref.shape[0]
    blk = pl.num_programs(0) - 1 - pl.program_id(0)

    @pl.when(blk % blocks_per_seq == blocks_per_seq - 1)
    def _():
        carry_ref[...] = jnp.zeros_like(carry_ref)

    lgb = _log_sigmoid(decb_ref[...])
    lane = lax.broadcasted_iota(jnp.int32, (SUBLANES, LANES), 1).astype(F32)
    k_decay = [jnp.exp(lgb[hd] * lane)[0:1, :] for hd in range(N_HEADS)]
    c_decay = [jnp.exp(lgb[hd] * float(CHUNK))[0:1, :] for hd in range(N_HEADS)]

    def kv_step(i, carry):
        for u in range(RET_UNROLL):
            c = i * RET_UNROLL + u
            rows = pl.ds(pl.multiple_of(c * CHUNK, CHUNK), CHUNK)
            for hd in range(N_HEADS):
                cols = slice(hd * LANES, (hd + 1) * LANES)
                kd = (kt_ref[c, cols, :].astype(F32) * k_decay[hd]).astype(BF16)
                kv_ref[c, hd] = jnp.dot(kd, v_ref[rows, cols], preferred_element_type=F32)
        return carry

    lax.fori_loop(0, n_chunks // RET_UNROLL, kv_step, 0)

    def scan_step(i, carry):
        c = n_chunks - 1 - i
        for hd in range(N_HEADS):
            state = carry_ref[hd]
            sb_ref[c, hd] = state.astype(BF16)
            carry_ref[hd] = state * c_decay[hd] + kv_ref[c, hd]
        return carry

    lax.fori_loop(0, n_chunks, scan_step, 0)

    @pl.when(pl.program_id(0) == 0)
    def _():
        lgf = _log_sigmoid(decf_ref[...])
        lgb = _log_sigmoid(decb_ref[...])
        i = lax.broadcasted_iota(jnp.int32, (CHUNK, CHUNK), 0).astype(F32)
        j = lax.broadcasted_iota(jnp.int32, (CHUNK, CHUNK), 1).astype(F32)
        diff = i - j
        for hd in range(N_HEADS):
            lf = lgf[hd][0:1, :]
            lb = lgb[hd][0:1, :]
            fwd = jnp.where(diff >= 0, jnp.exp(lf * jnp.maximum(diff, 0.0)), 0.0)
            bwd = jnp.where(diff <= 0, jnp.exp(lb * jnp.maximum(-diff, 0.0)), 0.0)
            mask_ref[hd] = fwd + bwd
            qdf_ref[hd] = jnp.exp(lf * (i + 1.0)).astype(BF16)
            qdb_ref[hd] = jnp.exp(lb * (float(CHUNK) - i)).astype(BF16)
        gs_ref[...] = gain_ref[...] * (float(LANES) ** 0.5)

    def one_chunk(c):
        rows = pl.ds(pl.multiple_of(c * CHUNK, CHUNK), CHUNK)
        for hd in range(N_HEADS):
            cols = slice(hd * LANES, (hd + 1) * LANES)
            qh = q_ref[rows, cols]
            scores = jnp.dot(qh, kt_ref[c, cols, :], preferred_element_type=F32) * mask_ref[hd]
            lhs = jnp.concatenate([scores.astype(BF16), qh * qdf_ref[hd], qh * qdb_ref[hd]], axis=1)
            rhs = jnp.concatenate([v_ref[rows, cols], sf_ref[c, hd], sb_ref[c, hd]], axis=0)
            d = jnp.dot(lhs, rhs, preferred_element_type=F32)
            ss = jnp.sum(d * d, axis=-1, keepdims=True)
            yn = d * lax.rsqrt(ss + float(LANES) * EPS) * gs_ref[:, cols]
            gate = g_ref[rows, cols].astype(F32)
            o_ref[rows, cols] = (gate * jax.nn.sigmoid(gate) * yn).astype(BF16)

    def body(i, carry):
        for u in range(RET_UNROLL):
            one_chunk(i * RET_UNROLL + u)
        return carry

    lax.fori_loop(0, n_chunks // RET_UNROLL, body, 0)


def _ret_main(dec_f, dec_b, gain, q, kt, v, g, sf, *, seq, blk):
    t = v.shape[0]
    nb = t // blk
    n_chunks = blk // CHUNK
    assert n_chunks % RET_UNROLL == 0
    row = lambda j: (nb - 1 - j, 0)
    return pl.pallas_call(
        functools.partial(_ret_main_kernel, blocks_per_seq=seq // blk),
        out_shape=jax.ShapeDtypeStruct((t, D_HALF), BF16),
        grid=(nb,),
        in_specs=[
            _const_spec((N_HEADS, SUBLANES, LANES)),
            _const_spec((N_HEADS, SUBLANES, LANES)),
            _const_spec((1, D_HALF)),
            pl.BlockSpec((blk, D_HALF), row),
            pl.BlockSpec((n_chunks, D_HALF, CHUNK), lambda j: (nb - 1 - j, 0, 0)),
            pl.BlockSpec((blk, D_HALF), row),
            pl.BlockSpec((blk, D_HALF), row),
            pl.BlockSpec((n_chunks, N_HEADS, LANES, LANES), lambda j: (nb - 1 - j, 0, 0, 0)),
        ],
        out_specs=pl.BlockSpec((blk, D_HALF), row),
        scratch_shapes=[
            pltpu.VMEM((N_HEADS, CHUNK, CHUNK), F32),
            pltpu.VMEM((N_HEADS, CHUNK, CHUNK), BF16),
            pltpu.VMEM((N_HEADS, CHUNK, CHUNK), BF16),
            pltpu.VMEM((1, D_HALF), F32),
            pltpu.VMEM((n_chunks, N_HEADS, LANES, LANES), F32),
            pltpu.VMEM((n_chunks, N_HEADS, LANES, LANES), BF16),
            pltpu.VMEM((N_HEADS, LANES, LANES), F32),
        ],
        compiler_params=_params(1),
        name="ret_main",
    )(dec_f, dec_b, gain, q, kt, v, g, sf)


def _slab_row(ref, *index):
    return jnp.concatenate([ref[(sl,) + index] for sl in range(ref.shape[0])], axis=1)


def _dft_stage_a_kernel(u_ref, m_ref, g_ref, y_ref):
    n_s2, r = u_ref.shape[1:3]
    words = jnp.concatenate([u_ref[sl].reshape(n_s2 * r, LANES) for sl in range(U_SLABS)], axis=1)
    lo, hi = _unpack_pair(words)
    u = jnp.concatenate([lo, hi], axis=1).astype(BF16)
    zz = [jnp.dot(u[:, g * LANES:(g + 1) * LANES], m_ref[g], preferred_element_type=F32) for g in range(N_HEADS)]
    zr = jnp.concatenate([z[:, :LANES] for z in zz], axis=1).astype(BF16)
    zi = jnp.concatenate([z[:, LANES:] for z in zz], axis=1).astype(BF16)
    for jj in range(n_s2):
        rows = slice(jj * r, (jj + 1) * r)
        rhs = jnp.concatenate([zr[rows], zi[rows]], axis=0)
        y = jnp.dot(g_ref[jj], rhs, preferred_element_type=F32)
        yw = _pack_pair(y[:r], y[r:])
        for sl in range(N_HEADS):
            _store_interleaved(y_ref.at[sl], jj, yw[:, sl * LANES:(sl + 1) * LANES])


def _dft_stage_b_kernel(y_ref, h_ref, o_ref):
    for kk in range(y_ref.shape[1]):
        yr, yi = _unpack_pair(_slab_row(y_ref, kk))
        rhs = jnp.concatenate([yr, yi], axis=0).astype(BF16)
        out = jnp.dot(h_ref[...], rhs, preferred_element_type=F32)
        for sl in range(N_HEADS):
            _store_interleaved(o_ref.at[sl], kk, out[:, sl * LANES:(sl + 1) * LANES])


def _dft_tables(seq, batch):
    n2 = LANES
    n1 = seq // n2
    idx = np.arange(n1)
    ang_a = 2.0 * np.pi * ((idx[:, None] * idx[None, :]) % n1) / n1
    ang_b = 2.0 * np.pi * (np.arange(n2)[:, None] * idx[None, :]) / seq
    ca, sa = jnp.asarray(np.cos(ang_a), F32)[None], jnp.asarray(np.sin(ang_a), F32)[None]
    cb, sb = jnp.asarray(np.cos(ang_b), F32)[:, :, None], jnp.asarray(np.sin(ang_b), F32)[:, :, None]
    c = ca * cb - sa * sb
    s = sa * cb + ca * sb
    eye = jnp.eye(batch, dtype=F32)[None, :, None, :, None]

    def per_sequence(m):
        return (eye * m[:, None, :, None, :]).reshape(n2, batch * n1, batch * n1)

    g = jnp.concatenate([jnp.concatenate([per_sequence(c), per_sequence(s)], axis=2),
                         jnp.concatenate([per_sequence(-s), per_sequence(c)], axis=2)], axis=1).astype(BF16)
    k2 = np.arange(n2)
    ang2 = 2.0 * np.pi * ((k2[:, None] * k2[None, :]) % n2) / n2
    scale = (seq * LANES) ** -0.5
    hm = jnp.asarray(np.concatenate([np.cos(ang2), np.sin(ang2)], axis=1) * scale, F32).astype(BF16)
    return g, hm


def _dft_two_stage(u, m, layer, g, hm):
    _, n2, r, _ = u.shape
    y = pl.pallas_call(
        _dft_stage_a_kernel,
        out_shape=jax.ShapeDtypeStruct((N_HEADS, r, n2, LANES), jnp.uint32),
        grid=(n2 // DFT_ROWS,),
        in_specs=[
            pl.BlockSpec((U_SLABS, DFT_ROWS, r, LANES), lambda j: (0, j, 0, 0)),
            _layer_spec(m.shape, layer),
            pl.BlockSpec((DFT_ROWS, 2 * r, 2 * r), lambda j: (j, 0, 0)),
        ],
        out_specs=pl.BlockSpec((N_HEADS, r, DFT_ROWS, LANES), lambda j: (0, 0, j, 0)),
        compiler_params=_params(1),
        name="dft_stage_a",
    )(u, m, g)
    return pl.pallas_call(
        _dft_stage_b_kernel,
        out_shape=jax.ShapeDtypeStruct((N_HEADS, n2, r, LANES), F32),
        grid=(r // DFT_ROWS,),
        in_specs=[
            pl.BlockSpec((N_HEADS, DFT_ROWS, n2, LANES), lambda i: (0, i, 0, 0)),
            _const_spec((n2, 2 * n2)),
        ],
        out_specs=pl.BlockSpec((N_HEADS, n2, DFT_ROWS, LANES), lambda i: (0, 0, i, 0)),
        compiler_params=_params(1),
        name="dft_stage_b",
    )(y, hm)


FF_CHUNK = 512


def _out_mlp_kernel(x_ref, fo_ref, ro_ref, wo_ref, ln2_ref, w1_ref, w2_ref, lnf_ref, o_ref, *, final):
    tm = x_ref.shape[0]
    fo = jnp.concatenate([fo_ref[sl].reshape(tm, LANES) for sl in range(N_HEADS)], axis=1).astype(BF16)
    x1 = x_ref[...] + jnp.dot(fo, wo_ref[:D_HALF], preferred_element_type=F32)
    x1 = x1 + jnp.dot(ro_ref[...], wo_ref[D_HALF:], preferred_element_type=F32)
    h = _rmsnorm(x1, ln2_ref[...]).astype(BF16)
    acc = None
    for c in range(w1_ref.shape[1] // FF_CHUNK):
        cols = slice(c * FF_CHUNK, (c + 1) * FF_CHUNK)
        a = jnp.dot(h, w1_ref[:, cols], preferred_element_type=F32)
        a = jnp.square(jnp.maximum(a, 0.0)).astype(BF16)
        part = jnp.dot(a, w2_ref[cols, :], preferred_element_type=F32)
        acc = part if acc is None else acc + part
    acc = acc + x1
    if final:
        acc = _rmsnorm(acc, lnf_ref[...])
    o_ref[...] = acc


def _out_mlp(x, fo, ro, wo, ln2, w1, w2, lnf, layer, *, seq, final):
    t, d = x.shape
    tm = 1024
    n1 = seq // LANES
    tiles_per_seq = seq // tm
    fo = fo.reshape(N_HEADS, LANES, t // seq, n1, LANES)
    row = lambda i: (i, 0)
    return pl.pallas_call(
        functools.partial(_out_mlp_kernel, final=final),
        out_shape=jax.ShapeDtypeStruct((t, d), F32),
        grid=(t // tm,),
        in_specs=[
            pl.BlockSpec((tm, d), row),
            pl.BlockSpec((N_HEADS, tm // n1, None, n1, LANES), lambda i: (0, i % tiles_per_seq, i // tiles_per_seq, 0, 0)),
            pl.BlockSpec((tm, D_HALF), row),
            _layer_spec(wo.shape, layer),
            _const_spec((1, d)),
            _layer_spec(w1.shape, layer),
            _layer_spec(w2.shape, layer),
            _const_spec((1, d)),
        ],
        out_specs=pl.BlockSpec((tm, d), row),
        compiler_params=_params(1),
        name="out_mlp",
    )(x, fo, ro, wo, ln2, w1, w2, lnf)


def _rope_tables(seq):
    half = LANES // 2
    inv = ROPE_THETA ** (-np.arange(half, dtype=np.float64) / half)
    ang = np.arange(seq, dtype=np.float64)[:, None] * inv[None, :]
    c, s = np.cos(ang), np.sin(ang)
    cos_t = np.concatenate([c, c], axis=1).astype(np.float32)
    sin_t = np.concatenate([-s, s], axis=1).astype(np.float32)
    return jnp.asarray(cos_t), jnp.asarray(sin_t)


def _bcast_heads(v):
    return jnp.broadcast_to(v.astype(F32)[:, None, None], (N_HEADS, SUBLANES, LANES))


def _trunk(x, weights, *, seq):
    (wproj, m_dft), wo, w1, w2, ln1, ln2, dec_f, dec_b, gain, lnf = weights
    depth = wproj.shape[0]
    cos_t, sin_t = _rope_tables(seq)
    g_dft, h_dft = _dft_tables(seq, x.shape[0] // seq)
    blk = min(seq, 2048)
    for l in range(depth):
        df, db = _bcast_heads(dec_f[l]), _bcast_heads(dec_b[l])
        u, q, kt, v, g, sf = _proj(x, ln1[l][None], wproj, l, cos_t, sin_t, df, seq=seq)
        ro = _ret_main(df, db, gain[l][None].astype(F32), q, kt, v, g, sf, seq=seq, blk=blk)
        fo = _dft_two_stage(u, m_dft, l, g_dft, h_dft)
        x = _out_mlp(x, fo, ro, wo, ln2[l][None], w1, w2, lnf[None], l, seq=seq, final=(l == depth - 1))
    return x


def kernel(x_prompt, x_sample, ln1, w_in, w_fmix, decay_fwd, decay_bwd, gn_gain, w_o, ln2, w_ff1, w_ff2, ln_final):
    d = x_prompt.shape[-1]
    weights = (_prep_proj_weights(w_in, w_fmix), w_o.astype(BF16), w_ff1.astype(BF16), w_ff2.astype(BF16),
               ln1, ln2, decay_fwd, decay_bwd, gn_gain, ln_final)
    y_prompt = _trunk(x_prompt.reshape(-1, d), weights, seq=x_prompt.shape[1])
    y_sample = _trunk(x_sample.reshape(-1, d), weights, seq=x_sample.shape[1])
    return y_prompt.reshape(x_prompt.shape), y_sample.reshape(x_sample.shape)
```

```python
import functools

import jax
import jax.numpy as jnp
import numpy as np
from jax import lax
from jax.experimental import pallas as pl
from jax.experimental.pallas import tpu as pltpu

F32 = jnp.float32
BF16 = jnp.bfloat16

LANES = 128
SUBLANES = 8
N_HEADS = 4
D_HALF = N_HEADS * LANES
CHUNK = LANES
U_SLABS = D_HALF // (2 * LANES)
ROPE_THETA = 10000.0
EPS = 1e-6
VMEM_LIMIT = 56 * 1024 * 1024
DFT_ROWS = 8


def _params(n_axes=1, vmem=VMEM_LIMIT):
    return pltpu.CompilerParams(dimension_semantics=("arbitrary",) * n_axes, vmem_limit_bytes=vmem)


def _store_interleaved(ref, a, val):
    rows, n, lanes = ref.shape
    ref.reshape(rows * n, lanes)[pl.ds(a, rows, stride=n), :] = val


def _const_spec(shape):
    zeros = (0,) * len(shape)
    return pl.BlockSpec(shape, lambda *_: zeros, pipeline_mode=pl.Buffered(1))


def _layer_spec(stacked_shape, layer):
    index = (layer,) + (0,) * (len(stacked_shape) - 1)
    return pl.BlockSpec((None,) + tuple(stacked_shape[1:]), lambda *_: index, pipeline_mode=pl.Buffered(1))


def _pack_pair(lo, hi):
    return pltpu.pack_elementwise([lo, hi], packed_dtype=BF16)


def _unpack_pair(words):
    lo = pltpu.unpack_elementwise(words, index=0, packed_dtype=BF16, unpacked_dtype=F32)
    hi = pltpu.unpack_elementwise(words, index=1, packed_dtype=BF16, unpacked_dtype=F32)
    return lo, hi


PROJ_U, PROJ_Q, PROJ_K, PROJ_V, PROJ_G = (b * D_HALF for b in range(5))


def _prep_proj_kernel(cs_ref, win_ref, wf_ref, o_ref, m_ref):
    n = pl.program_id(1)

    @pl.when(n == 0)
    def _():
        for g in range(N_HEADS):
            cw = jnp.dot(cs_ref[...], wf_ref[0, g], precision=lax.Precision.HIGHEST, preferred_element_type=F32)
            m_ref[0, g] = jnp.concatenate([cw[:LANES], cw[LANES:]], axis=1).astype(BF16)

    @pl.when(n == PROJ_V // D_HALF)
    def _():
        for g in range(N_HEADS):
            cols = slice(g * LANES, (g + 1) * LANES)
            w = win_ref[0, :, cols]
            o_ref[0, :, cols] = (w - jnp.mean(w, axis=-1, keepdims=True)).astype(BF16)

    @pl.when(n != PROJ_V // D_HALF)
    def _():
        o_ref[0] = win_ref[0].astype(BF16)


def _prep_proj_weights(w_in, w_fmix):
    depth, d, width = w_in.shape
    c = np.arange(LANES)
    ang = 2.0 * np.pi * ((c[:, None] * c[None, :]) % LANES) / LANES
    cs = jnp.asarray(np.concatenate([np.cos(ang), -np.sin(ang)], axis=0), F32)
    return pl.pallas_call(
        _prep_proj_kernel,
        out_shape=(
            jax.ShapeDtypeStruct((depth, d, width), BF16),
            jax.ShapeDtypeStruct((depth, N_HEADS, LANES, 2 * LANES), BF16),
        ),
        grid=(depth, width // D_HALF),
        in_specs=[
            pl.BlockSpec((2 * LANES, LANES), lambda l, n: (0, 0)),
            pl.BlockSpec((1, d, D_HALF), lambda l, n: (l, 0, n)),
            pl.BlockSpec((1, N_HEADS, LANES, LANES), lambda l, n: (l, 0, 0, 0)),
        ],
        out_specs=(
            pl.BlockSpec((1, d, D_HALF), lambda l, n: (l, 0, n)),
            pl.BlockSpec((1, N_HEADS, LANES, 2 * LANES), lambda l, n: (l, 0, 0, 0)),
        ),
        compiler_params=_params(2),
        name="prep_proj_weights",
    )(cs, w_in, w_fmix)


def _rmsnorm(x, gain):
    return x * lax.rsqrt(jnp.mean(x * x, axis=-1, keepdims=True) + EPS) * gain


def _log_sigmoid(x):
    return -(jnp.maximum(-x, 0.0) + jnp.log1p(jnp.exp(-jnp.abs(x))))


def _proj_kernel(x_ref, ln_ref, w_ref, cos_ref, sin_ref, decf_ref, u_ref, q_ref, kt_ref, v_ref, g_ref, sf_ref,
                 carry_ref, *, tiles_per_seq):
    tm = x_ref.shape[0]

    @pl.when(pl.program_id(0) % tiles_per_seq == 0)
    def _():
        carry_ref[...] = jnp.zeros_like(carry_ref)

    h = _rmsnorm(x_ref[...], ln_ref[...]).astype(BF16)

    def project(first_col):
        return jnp.dot(h, w_ref[:, first_col:first_col + D_HALF], preferred_element_type=F32)

    cos = cos_ref[...]
    sin = sin_ref[...]

    def rope(t):
        return t * cos + pltpu.roll(t, LANES // 2, 1) * sin

    kf = project(PROJ_K)
    v = project(PROJ_V).astype(BF16)
    v_ref[...] = v
    qf = project(PROJ_Q)
    k_scale = LANES ** -0.5
    lgf = _log_sigmoid(decf_ref[...])
    lane = lax.broadcasted_iota(jnp.int32, (SUBLANES, LANES), 1).astype(F32)
    for hd in range(N_HEADS):
        cols = slice(hd * LANES, (hd + 1) * LANES)
        q_ref[:, cols] = rope(qf[:, cols]).astype(BF16)
        kt = (rope(kf[:, cols]) * k_scale).T
        k_decay = jnp.exp(lgf[hd] * (float(CHUNK - 1) - lane))[0:1, :]
        c_decay = jnp.exp(lgf[hd] * float(CHUNK))[0:1, :]
        for c in range(tm // CHUNK):
            toks = slice(c * CHUNK, (c + 1) * CHUNK)
            kt_ref[c, cols, :] = kt[:, toks].astype(BF16)
            state = carry_ref[hd]
            sf_ref[c, hd] = state.astype(BF16)
            kv = jnp.dot((kt[:, toks] * k_decay).astype(BF16), v[toks, cols], preferred_element_type=F32)
            carry_ref[hd] = state * c_decay + kv

    u = project(PROJ_U)
    uw = _pack_pair(u[:, :D_HALF // 2], u[:, D_HALF // 2:])
    for sl in range(u_ref.shape[0]):
        for a in range(tm // LANES):
            _store_interleaved(u_ref.at[sl], a, uw[a * LANES:(a + 1) * LANES, sl * LANES:(sl + 1) * LANES])
    g_ref[...] = project(PROJ_G).astype(BF16)


def _proj(x, ln, w, layer, cos_t, sin_t, dec_f, *, seq):
    t, d = x.shape
    tm = SUBLANES * LANES
    pos_blocks = seq // tm
    n_chunks = tm // CHUNK
    row = lambda i: (i, 0)
    return pl.pallas_call(
        functools.partial(_proj_kernel, tiles_per_seq=pos_blocks),
        out_shape=(
            jax.ShapeDtypeStruct((U_SLABS, LANES, t // LANES, LANES), jnp.uint32),
            jax.ShapeDtypeStruct((t, D_HALF), BF16),
            jax.ShapeDtypeStruct((t // CHUNK, D_HALF, CHUNK), BF16),
            jax.ShapeDtypeStruct((t, D_HALF), BF16),
            jax.ShapeDtypeStruct((t, D_HALF), BF16),
            jax.ShapeDtypeStruct((t // CHUNK, N_HEADS, LANES, LANES), BF16),
        ),
        grid=(t // tm,),
        in_specs=[
            pl.BlockSpec((tm, d), row),
            _const_spec((1, d)),
            _layer_spec(w.shape, layer),
            pl.BlockSpec((tm, LANES), lambda i: (i % pos_blocks, 0)),
            pl.BlockSpec((tm, LANES), lambda i: (i % pos_blocks, 0)),
            _const_spec((N_HEADS, SUBLANES, LANES)),
        ],
        out_specs=(
            pl.BlockSpec((U_SLABS, LANES, tm // LANES, LANES), lambda i: (0, 0, i, 0)),
            pl.BlockSpec((tm, D_HALF), row),
            pl.BlockSpec((n_chunks, D_HALF, CHUNK), lambda i: (i, 0, 0)),
            pl.BlockSpec((tm, D_HALF), row),
            pl.BlockSpec((tm, D_HALF), row),
            pl.BlockSpec((n_chunks, N_HEADS, LANES, LANES), lambda i: (i, 0, 0, 0)),
        ),
        scratch_shapes=[pltpu.VMEM((N_HEADS, LANES, LANES), F32)],
        compiler_params=_params(1),
        name="proj",
    )(x, ln, w, cos_t, sin_t, dec_f)


RET_UNROLL = 8


def _ret_main_kernel(decf_ref, decb_ref, gain_ref, q_ref, kt_ref, v_ref, g_ref, sf_ref, o_ref,
                     mask_ref, qdf_ref, qdb_ref, gs_ref, kv_ref, sb_ref, carry_ref, *, blocks_per_seq):
    n_chunks = kt_ref.shape[0]
    blk = pl.num_programs(0) - 1 - pl.program_id(0)

    @pl.when(blk % blocks_per_seq == blocks_per_seq - 1)
    def _():
        carry_ref[...] = jnp.zeros_like(carry_ref)

    lgb = _log_sigmoid(decb_ref[...])
    lane = lax.broadcasted_iota(jnp.int32, (SUBLANES, LANES), 1).astype(F32)
    k_decay = [jnp.exp(lgb[hd] * lane)[0:1, :] for hd in range(N_HEADS)]
    c_decay = [jnp.exp(lgb[hd] * float(CHUNK))[0:1, :] for hd in range(N_HEADS)]

    def kv_step(i, carry):
        for u in range(RET_UNROLL):
            c = i * RET_UNROLL + u
            rows = pl.ds(pl.multiple_of(c * CHUNK, CHUNK), CHUNK)
            for hd in range(N_HEADS):
                cols = slice(hd * LANES, (hd + 1) * LANES)
                kd = (kt_ref[c, cols, :].astype(F32) * k_decay[hd]).astype(BF16)
                kv_ref[c, hd] = jnp.dot(kd, v_ref[rows, cols], preferred_element_type=F32)
        return carry

    lax.fori_loop(0, n_chunks // RET_UNROLL, kv_step, 0)

    def scan_step(i, carry):
        c = n_chunks - 1 - i
        for hd in range(N_HEADS):
            state = carry_ref[hd]
            sb_ref[c, hd] = state.astype(BF16)
            carry_ref[hd] = state * c_decay[hd] + kv_ref[c, hd]
        return carry

    lax.fori_loop(0, n_chunks, scan_step, 0)

    @pl.when(pl.program_id(0) == 0)
    def _():
        lgf = _log_sigmoid(decf_ref[...])
        lgb = _log_sigmoid(decb_ref[...])
        i = lax.broadcasted_iota(jnp.int32, (CHUNK, CHUNK), 0).astype(F32)
        j = lax.broadcasted_iota(jnp.int32, (CHUNK, CHUNK), 1).astype(F32)
        diff = i - j
        for hd in range(N_HEADS):
            lf = lgf[hd][0:1, :]
            lb = lgb[hd][0:1, :]
            fwd = jnp.where(diff >= 0, jnp.exp(lf * jnp.maximum(diff, 0.0)), 0.0)
            bwd = jnp.where(diff <= 0, jnp.exp(lb * jnp.maximum(-diff, 0.0)), 0.0)
            mask_ref[hd] = fwd + bwd
            qdf_ref[hd] = jnp.exp(lf * (i + 1.0)).astype(BF16)
            qdb_ref[hd] = jnp.exp(lb * (float(CHUNK) - i)).astype(BF16)
        gs_ref[...] = gain_ref[...] * (float(LANES) ** 0.5)

    def one_chunk(c):
        rows = pl.ds(pl.multiple_of(c * CHUNK, CHUNK), CHUNK)
        for hd in range(N_HEADS):
            cols = slice(hd * LANES, (hd + 1) * LANES)
            qh = q_ref[rows, cols]
            scores = jnp.dot(qh, kt_ref[c, cols, :], preferred_element_type=F32) * mask_ref[hd]
            lhs = jnp.concatenate([scores.astype(BF16), qh * qdf_ref[hd], qh * qdb_ref[hd]], axis=1)
            rhs = jnp.concatenate([v_ref[rows, cols], sf_ref[c, hd], sb_ref[c, hd]], axis=0)
            d = jnp.dot(lhs, rhs, preferred_element_type=F32)
            ss = jnp.sum(d * d, axis=-1, keepdims=True)
            yn = d * lax.rsqrt(ss + float(LANES) * EPS) * gs_ref[:, cols]
            gate = g_ref[rows, cols].astype(F32)
            o_ref[rows, cols] = (gate * jax.nn.sigmoid(gate) * yn).astype(BF16)

    def body(i, carry):
        for u in range(RET_UNROLL):
            one_chunk(i * RET_UNROLL + u)
        return carry

    lax.fori_loop(0, n_chunks // RET_UNROLL, body, 0)


def _ret_main(dec_f, dec_b, gain, q, kt, v, g, sf, *, seq, blk):
    t = v.shape[0]
    nb = t // blk
    n_chunks = blk // CHUNK
    assert n_chunks % RET_UNROLL == 0
    row = lambda j: (nb - 1 - j, 0)
    return pl.pallas_call(
        functools.partial(_ret_main_kernel, blocks_per_seq=seq // blk),
        out_shape=jax.ShapeDtypeStruct((t, D_HALF), BF16),
        grid=(nb,),
        in_specs=[
            _const_spec((N_HEADS, SUBLANES, LANES)),
            _const_spec((N_HEADS, SUBLANES, LANES)),
            _const_spec((1, D_HALF)),
            pl.BlockSpec((blk, D_HALF), row),
            pl.BlockSpec((n_chunks, D_HALF, CHUNK), lambda j: (nb - 1 - j, 0, 0)),
            pl.BlockSpec((blk, D_HALF), row),
            pl.BlockSpec((blk, D_HALF), row),
            pl.BlockSpec((n_chunks, N_HEADS, LANES, LANES), lambda j: (nb - 1 - j, 0, 0, 0)),
        ],
        out_specs=pl.BlockSpec((blk, D_HALF), row),
        scratch_shapes=[
            pltpu.VMEM((N_HEADS, CHUNK, CHUNK), F32),
            pltpu.VMEM((N_HEADS, CHUNK, CHUNK), BF16),
            pltpu.VMEM((N_HEADS, CHUNK, CHUNK), BF16),
            pltpu.VMEM((1, D_HALF), F32),
            pltpu.VMEM((n_chunks, N_HEADS, LANES, LANES), F32),
            pltpu.VMEM((n_chunks, N_HEADS, LANES, LANES), BF16),
            pltpu.VMEM((N_HEADS, LANES, LANES), F32),
        ],
        compiler_params=_params(1),
        name="ret_main",
    )(dec_f, dec_b, gain, q, kt, v, g, sf)


def _slab_row(ref, *index):
    return jnp.concatenate([ref[(sl,) + index] for sl in range(ref.shape[0])], axis=1)


def _dft_stage_a_kernel(u_ref, m_ref, g_ref, y_ref):
    n_s2, r = u_ref.shape[1:3]
    words = jnp.concatenate([u_ref[sl].reshape(n_s2 * r, LANES) for sl in range(U_SLABS)], axis=1)
    lo, hi = _unpack_pair(words)
    u = jnp.concatenate([lo, hi], axis=1).astype(BF16)
    zz = [jnp.dot(u[:, g * LANES:(g + 1) * LANES], m_ref[g], preferred_element_type=F32) for g in range(N_HEADS)]
    zr = jnp.concatenate([z[:, :LANES] for z in zz], axis=1).astype(BF16)
    zi = jnp.concatenate([z[:, LANES:] for z in zz], axis=1).astype(BF16)
    for jj in range(n_s2):
        rows = slice(jj * r, (jj + 1) * r)
        rhs = jnp.concatenate([zr[rows], zi[rows]], axis=0)
        y = jnp.dot(g_ref[jj], rhs, preferred_element_type=F32)
        yw = _pack_pair(y[:r], y[r:])
        for sl in range(N_HEADS):
            _store_interleaved(y_ref.at[sl], jj, yw[:, sl * LANES:(sl + 1) * LANES])


def _dft_stage_b_kernel(y_ref, h_ref, o_ref):
    for kk in range(y_ref.shape[1]):
        yr, yi = _unpack_pair(_slab_row(y_ref, kk))
        rhs = jnp.concatenate([yr, yi], axis=0).astype(BF16)
        out = jnp.dot(h_ref[...], rhs, preferred_element_type=F32)
        for sl in range(N_HEADS):
            _store_interleaved(o_ref.at[sl], kk, out[:, sl * LANES:(sl + 1) * LANES])


def _dft_tables(seq, batch):
    n2 = LANES
    n1 = seq // n2
    idx = np.arange(n1)
    ang_a = 2.0 * np.pi * ((idx[:, None] * idx[None, :]) % n1) / n1
    ang_b = 2.0 * np.pi * (np.arange(n2)[:, None] * idx[None, :]) / seq
    ca, sa = jnp.asarray(np.cos(ang_a), F32)[None], jnp.asarray(np.sin(ang_a), F32)[None]
    cb, sb = jnp.asarray(np.cos(ang_b), F32)[:, :, None], jnp.asarray(np.sin(ang_b), F32)[:, :, None]
    c = ca * cb - sa * sb
    s = sa * cb + ca * sb
    eye = jnp.eye(batch, dtype=F32)[None, :, None, :, None]

    def per_sequence(m):
        return (eye * m[:, None, :, None, :]).reshape(n2, batch * n1, batch * n1)

    g = jnp.concatenate([jnp.concatenate([per_sequence(c), per_sequence(s)], axis=2),
                         jnp.concatenate([per_sequence(-s), per_sequence(c)], axis=2)], axis=1).astype(BF16)
    k2 = np.arange(n2)
    ang2 = 2.0 * np.pi * ((k2[:, None] * k2[None, :]) % n2) / n2
    scale = (seq * LANES) ** -0.5
    hm = jnp.asarray(np.concatenate([np.cos(ang2), np.sin(ang2)], axis=1) * scale, F32).astype(BF16)
    return g, hm


def _dft_two_stage(u, m, layer, g, hm):
    _, n2, r, _ = u.shape
    y = pl.pallas_call(
        _dft_stage_a_kernel,
        out_shape=jax.ShapeDtypeStruct((N_HEADS, r, n2, LANES), jnp.uint32),
        grid=(n2 // DFT_ROWS,),
        in_specs=[
            pl.BlockSpec((U_SLABS, DFT_ROWS, r, LANES), lambda j: (0, j, 0, 0)),
            _layer_spec(m.shape, layer),
            pl.BlockSpec((DFT_ROWS, 2 * r, 2 * r), lambda j: (j, 0, 0)),
        ],
        out_specs=pl.BlockSpec((N_HEADS, r, DFT_ROWS, LANES), lambda j: (0, 0, j, 0)),
        compiler_params=_params(1),
        name="dft_stage_a",
    )(u, m, g)
    return pl.pallas_call(
        _dft_stage_b_kernel,
        out_shape=jax.ShapeDtypeStruct((N_HEADS, n2, r, LANES), F32),
        grid=(r // DFT_ROWS,),
        in_specs=[
            pl.BlockSpec((N_HEADS, DFT_ROWS, n2, LANES), lambda i: (0, i, 0, 0)),
            _const_spec((n2, 2 * n2)),
        ],
        out_specs=pl.BlockSpec((N_HEADS, n2, DFT_ROWS, LANES), lambda i: (0, 0, i, 0)),
        compiler_params=_params(1),
        name="dft_stage_b",
    )(y, hm)


FF_CHUNK = 1024


def _out_mlp_kernel(x_ref, fo_ref, ro_ref, wo_ref, ln2_ref, w1_ref, w2_ref, lnf_ref, *rest, final, n_cast):
    o_ref = rest[n_cast]
    for src_ref, dst_ref in zip(rest[:n_cast], rest[n_cast + 1:]):
        dst_ref[...] = src_ref[...].astype(BF16)
    tm = x_ref.shape[0]
    fo = jnp.concatenate([fo_ref[sl].reshape(tm, LANES) for sl in range(N_HEADS)], axis=1).astype(BF16)
    x1 = x_ref[...] + jnp.dot(fo, wo_ref[:D_HALF], preferred_element_type=F32)
    x1 = x1 + jnp.dot(ro_ref[...], wo_ref[D_HALF:], preferred_element_type=F32)
    h = _rmsnorm(x1, ln2_ref[...]).astype(BF16)
    acc = None
    for c in range(w1_ref.shape[1] // FF_CHUNK):
        cols = slice(c * FF_CHUNK, (c + 1) * FF_CHUNK)
        a = jnp.dot(h, w1_ref[:, cols], preferred_element_type=F32)
        a = jnp.square(jnp.maximum(a, 0.0)).astype(BF16)
        part = jnp.dot(a, w2_ref[cols, :], preferred_element_type=F32)
        acc = part if acc is None else acc + part
    acc = acc + x1
    if final:
        acc = _rmsnorm(acc, lnf_ref[...])
    o_ref[...] = acc


def _out_mlp(x, fo, ro, weights, ln2, lnf, *, seq, final, cast_next=None):
    t, d = x.shape
    tm = 512
    steps = t // tm
    n1 = seq // LANES
    tiles_per_seq = seq // tm
    fo = fo.reshape(N_HEADS, LANES, t // seq, n1, LANES)
    row = lambda i: (i, 0)
    wo, w1, w2 = weights
    cast_args, cast_in_specs, cast_out_specs, cast_out_shapes = [], [], [], []
    if cast_next is not None:
        stacked, nxt = cast_next
        for w in stacked:
            rows, cols = w.shape[1] // steps, w.shape[2]
            cast_args.append(w)
            cast_in_specs.append(pl.BlockSpec((None, rows, cols), lambda i: (nxt, i, 0)))
            cast_out_specs.append(pl.BlockSpec((rows, cols), row))
            cast_out_shapes.append(jax.ShapeDtypeStruct(w.shape[1:], BF16))
    out = pl.pallas_call(
        functools.partial(_out_mlp_kernel, final=final, n_cast=len(cast_args)),
        out_shape=[jax.ShapeDtypeStruct((t, d), F32)] + cast_out_shapes,
        grid=(steps,),
        in_specs=[
            pl.BlockSpec((tm, d), row),
            pl.BlockSpec((N_HEADS, tm // n1, None, n1, LANES), lambda i: (0, i % tiles_per_seq, i // tiles_per_seq, 0, 0)),
            pl.BlockSpec((tm, D_HALF), row),
            _const_spec(wo.shape),
            _const_spec((1, d)),
            _const_spec(w1.shape),
            _const_spec(w2.shape),
            _const_spec((1, d)),
        ] + cast_in_specs,
        out_specs=[pl.BlockSpec((tm, d), row)] + cast_out_specs,
        compiler_params=_params(1),
        name="out_mlp",
    )(x, fo, ro, wo, ln2, w1, w2, lnf, *cast_args)
    return out[0], tuple(out[1:])


def _rope_tables(seq):
    half = LANES // 2
    inv = ROPE_THETA ** (-np.arange(half, dtype=np.float64) / half)
    ang = np.arange(seq, dtype=np.float64)[:, None] * inv[None, :]
    c, s = np.cos(ang), np.sin(ang)
    cos_t = np.concatenate([c, c], axis=1).astype(np.float32)
    sin_t = np.concatenate([-s, s], axis=1).astype(np.float32)
    return jnp.asarray(cos_t), jnp.asarray(sin_t)


def _bcast_heads(v):
    return jnp.broadcast_to(v.astype(F32)[:, None, None], (N_HEADS, SUBLANES, LANES))


def _trunk(x, weights, mlp_weights, mlp_f32, *, seq):
    (wproj, m_dft), ln1, ln2, dec_f, dec_b, gain, lnf = weights
    depth = wproj.shape[0]
    cos_t, sin_t = _rope_tables(seq)
    g_dft, h_dft = _dft_tables(seq, x.shape[0] // seq)
    blk = min(seq, 2048)
    mlp_weights = list(mlp_weights)
    for l in range(depth):
        df, db = _bcast_heads(dec_f[l]), _bcast_heads(dec_b[l])
        u, q, kt, v, g, sf = _proj(x, ln1[l][None], wproj, l, cos_t, sin_t, df, seq=seq)
        ro = _ret_main(df, db, gain[l][None].astype(F32), q, kt, v, g, sf, seq=seq, blk=blk)
        fo = _dft_two_stage(u, m_dft, l, g_dft, h_dft)
        cast_next = (mlp_f32, l + 1) if len(mlp_weights) == l + 1 and l + 1 < depth else None
        x, nxt = _out_mlp(x, fo, ro, mlp_weights[l], ln2[l][None], lnf[None], seq=seq, final=(l == depth - 1),
                          cast_next=cast_next)
        if cast_next is not None:
            mlp_weights.append(nxt)
    return x, mlp_weights


def kernel(x_prompt, x_sample, ln1, w_in, w_fmix, decay_fwd, decay_bwd, gn_gain, w_o, ln2, w_ff1, w_ff2, ln_final):
    d = x_prompt.shape[-1]
    weights = (_prep_proj_weights(w_in, w_fmix), ln1, ln2, decay_fwd, decay_bwd, gn_gain, ln_final)
    mlp_f32 = (w_o, w_ff1, w_ff2)
    first = [tuple(w[0].astype(BF16) for w in mlp_f32)]
    y_prompt, mlp_weights = _trunk(x_prompt.reshape(-1, d), weights, first, mlp_f32, seq=x_prompt.shape[1])
    y_sample, _ = _trunk(x_sample.reshape(-1, d), weights, mlp_weights, mlp_f32, seq=x_sample.shape[1])
    return y_prompt.reshape(x_prompt.shape), y_sample.reshape(x_sample.shape)
```

```python
import functools

import jax
import jax.numpy as jnp
import numpy as np
from jax import lax
from jax.experimental import pallas as pl
from jax.experimental.pallas import tpu as pltpu

F32 = jnp.float32
BF16 = jnp.bfloat16

LANES = 128
SUBLANES = 8
N_HEADS = 4
D_HALF = N_HEADS * LANES
CHUNK = LANES
U_SLABS = D_HALF // (2 * LANES)
ROPE_THETA = 10000.0
EPS = 1e-6
VMEM_LIMIT = 56 * 1024 * 1024
DFT_ROWS = 8


def _params(n_axes=1, vmem=VMEM_LIMIT):
    return pltpu.CompilerParams(dimension_semantics=("arbitrary",) * n_axes, vmem_limit_bytes=vmem)


def _store_interleaved(ref, a, val):
    rows, n, lanes = ref.shape
    ref.reshape(rows * n, lanes)[pl.ds(a, rows, stride=n), :] = val


def _const_spec(shape):
    zeros = (0,) * len(shape)
    return pl.BlockSpec(shape, lambda *_: zeros, pipeline_mode=pl.Buffered(1))


def _layer_spec(stacked_shape, layer):
    index = (layer,) + (0,) * (len(stacked_shape) - 1)
    return pl.BlockSpec((None,) + tuple(stacked_shape[1:]), lambda *_: index, pipeline_mode=pl.Buffered(1))


def _pack_pair(lo, hi):
    return pltpu.pack_elementwise([lo, hi], packed_dtype=BF16)


def _unpack_pair(words):
    lo = pltpu.unpack_elementwise(words, index=0, packed_dtype=BF16, unpacked_dtype=F32)
    hi = pltpu.unpack_elementwise(words, index=1, packed_dtype=BF16, unpacked_dtype=F32)
    return lo, hi


PROJ_U, PROJ_Q, PROJ_K, PROJ_V, PROJ_G = (b * D_HALF for b in range(5))


def _prep_proj_kernel(cs_ref, win_ref, wf_ref, o_ref, m_ref):
    n = pl.program_id(1)

    @pl.when(n == 0)
    def _():
        for g in range(N_HEADS):
            cw = jnp.dot(cs_ref[...], wf_ref[0, g], precision=lax.Precision.HIGHEST, preferred_element_type=F32)
            m_ref[0, g] = jnp.concatenate([cw[:LANES], cw[LANES:]], axis=1).astype(BF16)

    @pl.when(n == PROJ_V // D_HALF)
    def _():
        for g in range(N_HEADS):
            cols = slice(g * LANES, (g + 1) * LANES)
            w = win_ref[0, :, cols]
            o_ref[0, :, cols] = (w - jnp.mean(w, axis=-1, keepdims=True)).astype(BF16)

    @pl.when(n != PROJ_V // D_HALF)
    def _():
        o_ref[0] = win_ref[0].astype(BF16)


def _prep_proj_weights(w_in, w_fmix):
    depth, d, width = w_in.shape
    c = np.arange(LANES)
    ang = 2.0 * np.pi * ((c[:, None] * c[None, :]) % LANES) / LANES
    cs = jnp.asarray(np.concatenate([np.cos(ang), -np.sin(ang)], axis=0), F32)
    return pl.pallas_call(
        _prep_proj_kernel,
        out_shape=(
            jax.ShapeDtypeStruct((depth, d, width), BF16),
            jax.ShapeDtypeStruct((depth, N_HEADS, LANES, 2 * LANES), BF16),
        ),
        grid=(depth, width // D_HALF),
        in_specs=[
            pl.BlockSpec((2 * LANES, LANES), lambda l, n: (0, 0)),
            pl.BlockSpec((1, d, D_HALF), lambda l, n: (l, 0, n)),
            pl.BlockSpec((1, N_HEADS, LANES, LANES), lambda l, n: (l, 0, 0, 0)),
        ],
        out_specs=(
            pl.BlockSpec((1, d, D_HALF), lambda l, n: (l, 0, n)),
            pl.BlockSpec((1, N_HEADS, LANES, 2 * LANES), lambda l, n: (l, 0, 0, 0)),
        ),
        compiler_params=_params(2),
        name="prep_proj_weights",
    )(cs, w_in, w_fmix)


def _rmsnorm(x, gain):
    return x * lax.rsqrt(jnp.mean(x * x, axis=-1, keepdims=True) + EPS) * gain


def _log_sigmoid(x):
    return -(jnp.maximum(-x, 0.0) + jnp.log1p(jnp.exp(-jnp.abs(x))))


def _proj_kernel(x_ref, ln_ref, w_ref, cos_ref, sin_ref, decf_ref, u_ref, q_ref, kt_ref, v_ref, g_ref, sf_ref,
                 carry_ref, *, tiles_per_seq):
    tm = x_ref.shape[0]

    @pl.when(pl.program_id(0) % tiles_per_seq == 0)
    def _():
        carry_ref[...] = jnp.zeros_like(carry_ref)

    h = _rmsnorm(x_ref[...], ln_ref[...]).astype(BF16)

    def project(first_col):
        return jnp.dot(h, w_ref[:, first_col:first_col + D_HALF], preferred_element_type=F32)

    cos = cos_ref[...]
    sin = sin_ref[...]

    def rope(t):
        return t * cos + pltpu.roll(t, LANES // 2, 1) * sin

    kf = project(PROJ_K)
    v = project(PROJ_V).astype(BF16)
    v_ref[...] = v
    qf = project(PROJ_Q)
    k_scale = LANES ** -0.5
    lgf = _log_sigmoid(decf_ref[...])
    lane = lax.broadcasted_iota(jnp.int32, (SUBLANES, LANES), 1).astype(F32)
    for hd in range(N_HEADS):
        cols = slice(hd * LANES, (hd + 1) * LANES)
        q_ref[:, cols] = rope(qf[:, cols]).astype(BF16)
        kt = (rope(kf[:, cols]) * k_scale).T
        k_decay = jnp.exp(lgf[hd] * (float(CHUNK - 1) - lane))[0:1, :]
        c_decay = jnp.exp(lgf[hd] * float(CHUNK))[0:1, :]
        for c in range(tm // CHUNK):
            toks = slice(c * CHUNK, (c + 1) * CHUNK)
            kt_ref[c, cols, :] = kt[:, toks].astype(BF16)
            state = carry_ref[hd]
            sf_ref[c, hd] = state.astype(BF16)
            kv = jnp.dot((kt[:, toks] * k_decay).astype(BF16), v[toks, cols], preferred_element_type=F32)
            carry_ref[hd] = state * c_decay + kv

    u = project(PROJ_U)
    uw = _pack_pair(u[:, :D_HALF // 2], u[:, D_HALF // 2:])
    for sl in range(u_ref.shape[0]):
        for a in range(tm // LANES):
            _store_interleaved(u_ref.at[sl], a, uw[a * LANES:(a + 1) * LANES, sl * LANES:(sl + 1) * LANES])
    g_ref[...] = project(PROJ_G).astype(BF16)


def _proj(x, ln, w, layer, cos_t, sin_t, dec_f, *, seq):
    t, d = x.shape
    tm = SUBLANES * LANES
    pos_blocks = seq // tm
    n_chunks = tm // CHUNK
    row = lambda i: (i, 0)
    return pl.pallas_call(
        functools.partial(_proj_kernel, tiles_per_seq=pos_blocks),
        out_shape=(
            jax.ShapeDtypeStruct((U_SLABS, LANES, t // LANES, LANES), jnp.uint32),
            jax.ShapeDtypeStruct((t, D_HALF), BF16),
            jax.ShapeDtypeStruct((t // CHUNK, D_HALF, CHUNK), BF16),
            jax.ShapeDtypeStruct((t, D_HALF), BF16),
            jax.ShapeDtypeStruct((t, D_HALF), BF16),
            jax.ShapeDtypeStruct((t // CHUNK, N_HEADS, LANES, LANES), BF16),
        ),
        grid=(t // tm,),
        in_specs=[
            pl.BlockSpec((tm, d), row),
            _const_spec((1, d)),
            _layer_spec(w.shape, layer),
            pl.BlockSpec((tm, LANES), lambda i: (i % pos_blocks, 0)),
            pl.BlockSpec((tm, LANES), lambda i: (i % pos_blocks, 0)),
            _const_spec((N_HEADS, SUBLANES, LANES)),
        ],
        out_specs=(
            pl.BlockSpec((U_SLABS, LANES, tm // LANES, LANES), lambda i: (0, 0, i, 0)),
            pl.BlockSpec((tm, D_HALF), row),
            pl.BlockSpec((n_chunks, D_HALF, CHUNK), lambda i: (i, 0, 0)),
            pl.BlockSpec((tm, D_HALF), row),
            pl.BlockSpec((tm, D_HALF), row),
            pl.BlockSpec((n_chunks, N_HEADS, LANES, LANES), lambda i: (i, 0, 0, 0)),
        ),
        scratch_shapes=[pltpu.VMEM((N_HEADS, LANES, LANES), F32)],
        compiler_params=_params(1),
        name="proj",
    )(x, ln, w, cos_t, sin_t, dec_f)


RET_UNROLL = 16


def _ret_main_kernel(decf_ref, decb_ref, gain_ref, q_ref, kt_ref, v_ref, g_ref, sf_ref, o_ref,
                     mask_ref, qdf_ref, qdb_ref, gs_ref, kv_ref, sb_ref, carry_ref, *, blocks_per_seq):
    n_chunks = kt_ref.shape[0]
    blk = pl.num_programs(0) - 1 - pl.program_id(0)

    @pl.when(blk % blocks_per_seq == blocks_per_seq - 1)
    def _():
        carry_ref[...] = jnp.zeros_like(carry_ref)

    lgb = _log_sigmoid(decb_ref[...])
    lane = lax.broadcasted_iota(jnp.int32, (SUBLANES, LANES), 1).astype(F32)
    k_decay = [jnp.exp(lgb[hd] * lane)[0:1, :] for hd in range(N_HEADS)]
    c_decay = [jnp.exp(lgb[hd] * float(CHUNK))[0:1, :] for hd in range(N_HEADS)]

    def kv_step(i, carry):
        for u in range(RET_UNROLL):
            c = i * RET_UNROLL + u
            rows = pl.ds(pl.multiple_of(c * CHUNK, CHUNK), CHUNK)
            for hd in range(N_HEADS):
                cols = slice(hd * LANES, (hd + 1) * LANES)
                kd = (kt_ref[c, cols, :].astype(F32) * k_decay[hd]).astype(BF16)
                kv_ref[c, hd] = jnp.dot(kd, v_ref[rows, cols], preferred_element_type=F32)
        return carry

    lax.fori_loop(0, n_chunks // RET_UNROLL, kv_step, 0)

    def scan_step(i, carry):
        c = n_chunks - 1 - i
        for hd in range(N_HEADS):
            state = carry_ref[hd]
            sb_ref[c, hd] = state.astype(BF16)
            carry_ref[hd] = state * c_decay[hd] + kv_ref[c, hd]
        return carry

    lax.fori_loop(0, n_chunks, scan_step, 0)

    @pl.when(pl.program_id(0) == 0)
    def _():
        lgf = _log_sigmoid(decf_ref[...])
        lgb = _log_sigmoid(decb_ref[...])
        i = lax.broadcasted_iota(jnp.int32, (CHUNK, CHUNK), 0).astype(F32)
        j = lax.broadcasted_iota(jnp.int32, (CHUNK, CHUNK), 1).astype(F32)
        diff = i - j
        for hd in range(N_HEADS):
            lf = lgf[hd][0:1, :]
            lb = lgb[hd][0:1, :]
            fwd = jnp.where(diff >= 0, jnp.exp(lf * jnp.maximum(diff, 0.0)), 0.0)
            bwd = jnp.where(diff <= 0, jnp.exp(lb * jnp.maximum(-diff, 0.0)), 0.0)
            mask_ref[hd] = fwd + bwd
            qdf_ref[hd] = jnp.exp(lf * (i + 1.0)).astype(BF16)
            qdb_ref[hd] = jnp.exp(lb * (float(CHUNK) - i)).astype(BF16)
        gs_ref[...] = gain_ref[...] * (float(LANES) ** 0.5)

    def one_chunk(c):
        rows = pl.ds(pl.multiple_of(c * CHUNK, CHUNK), CHUNK)
        for hd in range(N_HEADS):
            cols = slice(hd * LANES, (hd + 1) * LANES)
            qh = q_ref[rows, cols]
            scores = jnp.dot(qh, kt_ref[c, cols, :], preferred_element_type=F32) * mask_ref[hd]
            lhs = jnp.concatenate([scores.astype(BF16), qh * qdf_ref[hd], qh * qdb_ref[hd]], axis=1)
            rhs = jnp.concatenate([v_ref[rows, cols], sf_ref[c, hd], sb_ref[c, hd]], axis=0)
            d = jnp.dot(lhs, rhs, preferred_element_type=F32)
            ss = jnp.sum(d * d, axis=-1, keepdims=True)
            yn = d * lax.rsqrt(ss + float(LANES) * EPS) * gs_ref[:, cols]
            gate = g_ref[rows, cols].astype(F32)
            o_ref[rows, cols] = (gate * jax.nn.sigmoid(gate) * yn).astype(BF16)

    def body(i, carry):
        for u in range(RET_UNROLL):
            one_chunk(i * RET_UNROLL + u)
        return carry

    lax.fori_loop(0, n_chunks // RET_UNROLL, body, 0)


def _ret_main(dec_f, dec_b, gain, q, kt, v, g, sf, *, seq, blk):
    t = v.shape[0]
    nb = t // blk
    n_chunks = blk // CHUNK
    assert n_chunks % RET_UNROLL == 0
    row = lambda j: (nb - 1 - j, 0)
    return pl.pallas_call(
        functools.partial(_ret_main_kernel, blocks_per_seq=seq // blk),
        out_shape=jax.ShapeDtypeStruct((t, D_HALF), BF16),
        grid=(nb,),
        in_specs=[
            _const_spec((N_HEADS, SUBLANES, LANES)),
            _const_spec((N_HEADS, SUBLANES, LANES)),
            _const_spec((1, D_HALF)),
            pl.BlockSpec((blk, D_HALF), row),
            pl.BlockSpec((n_chunks, D_HALF, CHUNK), lambda j: (nb - 1 - j, 0, 0)),
            pl.BlockSpec((blk, D_HALF), row),
            pl.BlockSpec((blk, D_HALF), row),
            pl.BlockSpec((n_chunks, N_HEADS, LANES, LANES), lambda j: (nb - 1 - j, 0, 0, 0)),
        ],
        out_specs=pl.BlockSpec((blk, D_HALF), row),
        scratch_shapes=[
            pltpu.VMEM((N_HEADS, CHUNK, CHUNK), F32),
            pltpu.VMEM((N_HEADS, CHUNK, CHUNK), BF16),
            pltpu.VMEM((N_HEADS, CHUNK, CHUNK), BF16),
            pltpu.VMEM((1, D_HALF), F32),
            pltpu.VMEM((n_chunks, N_HEADS, LANES, LANES), F32),
            pltpu.VMEM((n_chunks, N_HEADS, LANES, LANES), BF16),
            pltpu.VMEM((N_HEADS, LANES, LANES), F32),
        ],
        compiler_params=_params(1),
        name="ret_main",
    )(dec_f, dec_b, gain, q, kt, v, g, sf)


def _slab_row(ref, *index):
    return jnp.concatenate([ref[(sl,) + index] for sl in range(ref.shape[0])], axis=1)


def _dft_stage_a_kernel(u_ref, m_ref, g_ref, y_ref):
    n_s2, r = u_ref.shape[1:3]
    words = jnp.concatenate([u_ref[sl].reshape(n_s2 * r, LANES) for sl in range(U_SLABS)], axis=1)
    lo, hi = _unpack_pair(words)
    u = jnp.concatenate([lo, hi], axis=1).astype(BF16)
    zz = [jnp.dot(u[:, g * LANES:(g + 1) * LANES], m_ref[g], preferred_element_type=F32) for g in range(N_HEADS)]
    zr = jnp.concatenate([z[:, :LANES] for z in zz], axis=1).astype(BF16)
    zi = jnp.concatenate([z[:, LANES:] for z in zz], axis=1).astype(BF16)
    for jj in range(n_s2):
        rows = slice(jj * r, (jj + 1) * r)
        rhs = jnp.concatenate([zr[rows], zi[rows]], axis=0)
        y = jnp.dot(g_ref[jj], rhs, preferred_element_type=F32)
        yw = _pack_pair(y[:r], y[r:])
        for sl in range(N_HEADS):
            _store_interleaved(y_ref.at[sl], jj, yw[:, sl * LANES:(sl + 1) * LANES])


def _dft_stage_b_kernel(y_ref, h_ref, o_ref):
    for kk in range(y_ref.shape[1]):
        yr, yi = _unpack_pair(_slab_row(y_ref, kk))
        rhs = jnp.concatenate([yr, yi], axis=0).astype(BF16)
        out = jnp.dot(h_ref[...], rhs, preferred_element_type=F32)
        for sl in range(N_HEADS):
            _store_interleaved(o_ref.at[sl], kk, out[:, sl * LANES:(sl + 1) * LANES])


def _dft_tables(seq, batch):
    n2 = LANES
    n1 = seq // n2
    r = batch * n1
    idx = np.arange(2 * r)
    part, seq_id, digit = idx // r, (idx % r) // n1, idx % n1
    theta_a = 2.0 * np.pi * ((digit[:, None] * digit[None, :]) % n1) / n1
    theta_a = theta_a + (part[:, None] - part[None, :]) * (np.pi / 2.0)
    same_seq = seq_id[:, None] == seq_id[None, :]
    theta_b = 2.0 * np.pi * (np.arange(n2)[:, None] * digit[None, :]) / seq
    a_cos, a_sin = (jnp.asarray(f(theta_a) * same_seq, F32)[None] for f in (np.cos, np.sin))
    b_cos, b_sin = (jnp.asarray(f(theta_b), F32)[:, :, None] for f in (np.cos, np.sin))
    g = (a_cos * b_cos - a_sin * b_sin).astype(BF16)
    k2 = np.arange(n2)
    ang2 = 2.0 * np.pi * ((k2[:, None] * k2[None, :]) % n2) / n2
    scale = (seq * LANES) ** -0.5
    hm = jnp.asarray(np.concatenate([np.cos(ang2), np.sin(ang2)], axis=1) * scale, F32).astype(BF16)
    return g, hm


def _dft_two_stage(u, m, layer, g, hm):
    _, n2, r, _ = u.shape
    y = pl.pallas_call(
        _dft_stage_a_kernel,
        out_shape=jax.ShapeDtypeStruct((N_HEADS, r, n2, LANES), jnp.uint32),
        grid=(n2 // DFT_ROWS,),
        in_specs=[
            pl.BlockSpec((U_SLABS, DFT_ROWS, r, LANES), lambda j: (0, j, 0, 0)),
            _layer_spec(m.shape, layer),
            pl.BlockSpec((DFT_ROWS, 2 * r, 2 * r), lambda j: (j, 0, 0)),
        ],
        out_specs=pl.BlockSpec((N_HEADS, r, DFT_ROWS, LANES), lambda j: (0, 0, j, 0)),
        compiler_params=_params(1),
        name="dft_stage_a",
    )(u, m, g)
    return pl.pallas_call(
        _dft_stage_b_kernel,
        out_shape=jax.ShapeDtypeStruct((N_HEADS, n2, r, LANES), F32),
        grid=(r // DFT_ROWS,),
        in_specs=[
            pl.BlockSpec((N_HEADS, DFT_ROWS, n2, LANES), lambda i: (0, i, 0, 0)),
            _const_spec((n2, 2 * n2)),
        ],
        out_specs=pl.BlockSpec((N_HEADS, n2, DFT_ROWS, LANES), lambda i: (0, 0, i, 0)),
        compiler_params=_params(1),
        name="dft_stage_b",
    )(y, hm)


FF_CHUNK = 1024


def _out_mlp_kernel(x_ref, fo_ref, ro_ref, wo_ref, ln2_ref, w1_ref, w2_ref, lnf_ref, *rest, final, n_cast):
    o_ref = rest[n_cast]
    for src_ref, dst_ref in zip(rest[:n_cast], rest[n_cast + 1:]):
        dst_ref[...] = src_ref[...].astype(BF16)
    tm = x_ref.shape[0]
    fo = jnp.concatenate([fo_ref[sl].reshape(tm, LANES) for sl in range(N_HEADS)], axis=1).astype(BF16)
    x1 = x_ref[...] + jnp.dot(fo, wo_ref[:D_HALF], preferred_element_type=F32)
    x1 = x1 + jnp.dot(ro_ref[...], wo_ref[D_HALF:], preferred_element_type=F32)
    h = _rmsnorm(x1, ln2_ref[...]).astype(BF16)
    acc = None
    for c in range(w1_ref.shape[1] // FF_CHUNK):
        cols = slice(c * FF_CHUNK, (c + 1) * FF_CHUNK)
        a = jnp.dot(h, w1_ref[:, cols], preferred_element_type=F32)
        a = jnp.square(jnp.maximum(a, 0.0)).astype(BF16)
        part = jnp.dot(a, w2_ref[cols, :], preferred_element_type=F32)
        acc = part if acc is None else acc + part
    acc = acc + x1
    if final:
        acc = _rmsnorm(acc, lnf_ref[...])
    o_ref[...] = acc


def _out_mlp(x, fo, ro, weights, ln2, lnf, *, seq, final, cast_next=None):
    t, d = x.shape
    tm = 512
    steps = t // tm
    n1 = seq // LANES
    tiles_per_seq = seq // tm
    fo = fo.reshape(N_HEADS, LANES, t // seq, n1, LANES)
    row = lambda i: (i, 0)
    wo, w1, w2 = weights
    cast_args, cast_in_specs, cast_out_specs, cast_out_shapes = [], [], [], []
    if cast_next is not None:
        stacked, nxt = cast_next
        for w in stacked:
            rows, cols = w.shape[1] // steps, w.shape[2]
            cast_args.append(w)
            cast_in_specs.append(pl.BlockSpec((None, rows, cols), lambda i: (nxt, i, 0)))
            cast_out_specs.append(pl.BlockSpec((rows, cols), row))
            cast_out_shapes.append(jax.ShapeDtypeStruct(w.shape[1:], BF16))
    out = pl.pallas_call(
        functools.partial(_out_mlp_kernel, final=final, n_cast=len(cast_args)),
        out_shape=[jax.ShapeDtypeStruct((t, d), F32)] + cast_out_shapes,
        grid=(steps,),
        in_specs=[
            pl.BlockSpec((tm, d), row),
            pl.BlockSpec((N_HEADS, tm // n1, None, n1, LANES), lambda i: (0, i % tiles_per_seq, i // tiles_per_seq, 0, 0)),
            pl.BlockSpec((tm, D_HALF), row),
            _const_spec(wo.shape),
            _const_spec((1, d)),
            _const_spec(w1.shape),
            _const_spec(w2.shape),
            _const_spec((1, d)),
        ] + cast_in_specs,
        out_specs=[pl.BlockSpec((tm, d), row)] + cast_out_specs,
        compiler_params=_params(1),
        name="out_mlp",
    )(x, fo, ro, wo, ln2, w1, w2, lnf, *cast_args)
    return out[0], tuple(out[1:])


def _rope_tables(seq):
    half = LANES // 2
    inv = ROPE_THETA ** (-np.arange(half, dtype=np.float64) / half)
    ang = np.arange(seq, dtype=np.float64)[:, None] * inv[None, :]
    c, s = np.cos(ang), np.sin(ang)
    cos_t = np.concatenate([c, c], axis=1).astype(np.float32)
    sin_t = np.concatenate([-s, s], axis=1).astype(np.float32)
    return jnp.asarray(cos_t), jnp.asarray(sin_t)


def _bcast_heads(v):
    return jnp.broadcast_to(v.astype(F32)[:, None, None], (N_HEADS, SUBLANES, LANES))


def _trunk(x, weights, mlp_weights, mlp_f32, *, seq):
    (wproj, m_dft), ln1, ln2, dec_f, dec_b, gain, lnf = weights
    depth = wproj.shape[0]
    cos_t, sin_t = _rope_tables(seq)
    g_dft, h_dft = _dft_tables(seq, x.shape[0] // seq)
    blk = min(seq, 2048)
    mlp_weights = list(mlp_weights)
    for l in range(depth):
        df, db = _bcast_heads(dec_f[l]), _bcast_heads(dec_b[l])
        u, q, kt, v, g, sf = _proj(x, ln1[l][None], wproj, l, cos_t, sin_t, df, seq=seq)
        ro = _ret_main(df, db, gain[l][None].astype(F32), q, kt, v, g, sf, seq=seq, blk=blk)
        fo = _dft_two_stage(u, m_dft, l, g_dft, h_dft)
        cast_next = (mlp_f32, l + 1) if len(mlp_weights) == l + 1 and l + 1 < depth else None
        x, nxt = _out_mlp(x, fo, ro, mlp_weights[l], ln2[l][None], lnf[None], seq=seq, final=(l == depth - 1),
                          cast_next=cast_next)
        if cast_next is not None:
            mlp_weights.append(nxt)
    return x, mlp_weights


def kernel(x_prompt, x_sample, ln1, w_in, w_fmix, decay_fwd, decay_bwd, gn_gain, w_o, ln2, w_ff1, w_ff2, ln_final):
    d = x_prompt.shape[-1]
    weights = (_prep_proj_weights(w_in, w_fmix), ln1, ln2, decay_fwd, decay_bwd, gn_gain, ln_final)
    mlp_f32 = (w_o, w_ff1, w_ff2)
    first = [tuple(w[0].astype(BF16) for w in mlp_f32)]
    y_prompt, mlp_weights = _trunk(x_prompt.reshape(-1, d), weights, first, mlp_f32, seq=x_prompt.shape[1])
    y_sample, _ = _trunk(x_sample.reshape(-1, d), weights, mlp_weights, mlp_f32, seq=x_sample.shape[1])
    return y_prompt.reshape(x_prompt.shape), y_sample.reshape(x_sample.shape)
```

```python
import functools

import jax
import jax.numpy as jnp
import numpy as np
from jax import lax
from jax.experimental import pallas as pl
from jax.experimental.pallas import tpu as pltpu

F32 = jnp.float32
BF16 = jnp.bfloat16

LANES = 128
SUBLANES = 8
N_HEADS = 4
D_HALF = N_HEADS * LANES
CHUNK = LANES
U_SLABS = D_HALF // (2 * LANES)
ROPE_THETA = 10000.0
EPS = 1e-6
VMEM_LIMIT = 56 * 1024 * 1024
DFT_ROWS = 8


def _params(n_axes=1, vmem=VMEM_LIMIT):
    return pltpu.CompilerParams(dimension_semantics=("arbitrary",) * n_axes, vmem_limit_bytes=vmem)


def _store_interleaved(ref, a, val):
    rows, n, lanes = ref.shape
    ref.reshape(rows * n, lanes)[pl.ds(a, rows, stride=n), :] = val


def _const_spec(shape):
    zeros = (0,) * len(shape)
    return pl.BlockSpec(shape, lambda *_: zeros, pipeline_mode=pl.Buffered(1))


def _layer_spec(stacked_shape, layer):
    index = (layer,) + (0,) * (len(stacked_shape) - 1)
    return pl.BlockSpec((None,) + tuple(stacked_shape[1:]), lambda *_: index, pipeline_mode=pl.Buffered(1))


def _pack_pair(lo, hi):
    return pltpu.pack_elementwise([lo, hi], packed_dtype=BF16)


def _unpack_pair(words):
    lo = pltpu.unpack_elementwise(words, index=0, packed_dtype=BF16, unpacked_dtype=F32)
    hi = pltpu.unpack_elementwise(words, index=1, packed_dtype=BF16, unpacked_dtype=F32)
    return lo, hi


PROJ_U, PROJ_Q, PROJ_K, PROJ_V, PROJ_G = (b * D_HALF for b in range(5))


def _prep_proj_kernel(cs_ref, win_ref, wf_ref, o_ref, m_ref):
    n = pl.program_id(1)

    @pl.when(n == 0)
    def _():
        for g in range(N_HEADS):
            cw = jnp.dot(cs_ref[...], wf_ref[0, g], precision=lax.Precision.HIGHEST, preferred_element_type=F32)
            m_ref[0, g] = jnp.concatenate([cw[:LANES], cw[LANES:]], axis=1).astype(BF16)

    @pl.when(n == PROJ_V // D_HALF)
    def _():
        for g in range(N_HEADS):
            cols = slice(g * LANES, (g + 1) * LANES)
            w = win_ref[0, :, cols]
            o_ref[0, :, cols] = (w - jnp.mean(w, axis=-1, keepdims=True)).astype(BF16)

    @pl.when(n != PROJ_V // D_HALF)
    def _():
        o_ref[0] = win_ref[0].astype(BF16)


def _prep_proj_weights(w_in, w_fmix):
    depth, d, width = w_in.shape
    c = np.arange(LANES)
    ang = 2.0 * np.pi * ((c[:, None] * c[None, :]) % LANES) / LANES
    cs = jnp.asarray(np.concatenate([np.cos(ang), -np.sin(ang)], axis=0), F32)
    return pl.pallas_call(
        _prep_proj_kernel,
        out_shape=(
            jax.ShapeDtypeStruct((depth, d, width), BF16),
            jax.ShapeDtypeStruct((depth, N_HEADS, LANES, 2 * LANES), BF16),
        ),
        grid=(depth, width // D_HALF),
        in_specs=[
            pl.BlockSpec((2 * LANES, LANES), lambda l, n: (0, 0)),
            pl.BlockSpec((1, d, D_HALF), lambda l, n: (l, 0, n)),
            pl.BlockSpec((1, N_HEADS, LANES, LANES), lambda l, n: (l, 0, 0, 0)),
        ],
        out_specs=(
            pl.BlockSpec((1, d, D_HALF), lambda l, n: (l, 0, n)),
            pl.BlockSpec((1, N_HEADS, LANES, 2 * LANES), lambda l, n: (l, 0, 0, 0)),
        ),
        compiler_params=_params(2),
        name="prep_proj_weights",
    )(cs, w_in, w_fmix)


def _rmsnorm(x, gain):
    return x * lax.rsqrt(jnp.mean(x * x, axis=-1, keepdims=True) + EPS) * gain


def _log_sigmoid(x):
    return -(jnp.maximum(-x, 0.0) + jnp.log1p(jnp.exp(-jnp.abs(x))))


def _proj_kernel(x_ref, ln_ref, w_ref, cos_ref, sin_ref, decf_ref, u_ref, q_ref, kt_ref, v_ref, g_ref, sf_ref,
                 carry_ref, *, tiles_per_seq):
    tm = x_ref.shape[0]

    @pl.when(pl.program_id(0) % tiles_per_seq == 0)
    def _():
        carry_ref[...] = jnp.zeros_like(carry_ref)

    h = _rmsnorm(x_ref[...], ln_ref[...]).astype(BF16)

    def project(first_col):
        return jnp.dot(h, w_ref[:, first_col:first_col + D_HALF], preferred_element_type=F32)

    cos = cos_ref[...]
    sin = sin_ref[...]

    def rope(t):
        return t * cos + pltpu.roll(t, LANES // 2, 1) * sin

    kf = project(PROJ_K)
    v = project(PROJ_V).astype(BF16)
    v_ref[...] = v
    qf = project(PROJ_Q)
    k_scale = LANES ** -0.5
    lgf = _log_sigmoid(decf_ref[...])
    lane = lax.broadcasted_iota(jnp.int32, (SUBLANES, LANES), 1).astype(F32)
    for hd in range(N_HEADS):
        cols = slice(hd * LANES, (hd + 1) * LANES)
        q_ref[:, cols] = rope(qf[:, cols]).astype(BF16)
        kt = (rope(kf[:, cols]) * k_scale).T
        k_decay = jnp.exp(lgf[hd] * (float(CHUNK - 1) - lane))[0:1, :]
        c_decay = jnp.exp(lgf[hd] * float(CHUNK))[0:1, :]
        for c in range(tm // CHUNK):
            toks = slice(c * CHUNK, (c + 1) * CHUNK)
            kt_ref[c, cols, :] = kt[:, toks].astype(BF16)
            state = carry_ref[hd]
            sf_ref[c, hd] = state.astype(BF16)
            kv = jnp.dot((kt[:, toks] * k_decay).astype(BF16), v[toks, cols], preferred_element_type=F32)
            carry_ref[hd] = state * c_decay + kv

    u = project(PROJ_U)
    uw = _pack_pair(u[:, :D_HALF // 2], u[:, D_HALF // 2:])
    for sl in range(u_ref.shape[0]):
        for a in range(tm // LANES):
            _store_interleaved(u_ref.at[sl, 0], a, uw[a * LANES:(a + 1) * LANES, sl * LANES:(sl + 1) * LANES])
    g_ref[...] = project(PROJ_G).astype(BF16)


def _proj(x, ln, w, layer, cos_t, sin_t, dec_f, *, seq):
    t, d = x.shape
    tm = SUBLANES * LANES
    pos_blocks = seq // tm
    n_chunks = tm // CHUNK
    row = lambda i: (i, 0)
    return pl.pallas_call(
        functools.partial(_proj_kernel, tiles_per_seq=pos_blocks),
        out_shape=(
            jax.ShapeDtypeStruct((U_SLABS, t // tm, LANES, tm // LANES, LANES), jnp.uint32),
            jax.ShapeDtypeStruct((t, D_HALF), BF16),
            jax.ShapeDtypeStruct((t // CHUNK, D_HALF, CHUNK), BF16),
            jax.ShapeDtypeStruct((t, D_HALF), BF16),
            jax.ShapeDtypeStruct((t, D_HALF), BF16),
            jax.ShapeDtypeStruct((t // CHUNK, N_HEADS, LANES, LANES), BF16),
        ),
        grid=(t // tm,),
        in_specs=[
            pl.BlockSpec((tm, d), row),
            _const_spec((1, d)),
            _layer_spec(w.shape, layer),
            pl.BlockSpec((tm, LANES), lambda i: (i % pos_blocks, 0)),
            pl.BlockSpec((tm, LANES), lambda i: (i % pos_blocks, 0)),
            _const_spec((N_HEADS, SUBLANES, LANES)),
        ],
        out_specs=(
            pl.BlockSpec((U_SLABS, 1, LANES, tm // LANES, LANES), lambda i: (0, i, 0, 0, 0)),
            pl.BlockSpec((tm, D_HALF), row),
            pl.BlockSpec((n_chunks, D_HALF, CHUNK), lambda i: (i, 0, 0)),
            pl.BlockSpec((tm, D_HALF), row),
            pl.BlockSpec((tm, D_HALF), row),
            pl.BlockSpec((n_chunks, N_HEADS, LANES, LANES), lambda i: (i, 0, 0, 0)),
        ),
        scratch_shapes=[pltpu.VMEM((N_HEADS, LANES, LANES), F32)],
        compiler_params=_params(1),
        name="proj",
    )(x, ln, w, cos_t, sin_t, dec_f)


RET_UNROLL = 16


def _ret_main_kernel(decf_ref, decb_ref, gain_ref, q_ref, kt_ref, v_ref, g_ref, sf_ref, o_ref,
                     mask_ref, qdf_ref, qdb_ref, gs_ref, kv_ref, sb_ref, carry_ref, *, blocks_per_seq):
    n_chunks = kt_ref.shape[0]
    blk = pl.num_programs(0) - 1 - pl.program_id(0)

    @pl.when(blk % blocks_per_seq == blocks_per_seq - 1)
    def _():
        carry_ref[...] = jnp.zeros_like(carry_ref)

    lgb = _log_sigmoid(decb_ref[...])
    lane = lax.broadcasted_iota(jnp.int32, (SUBLANES, LANES), 1).astype(F32)
    k_decay = [jnp.exp(lgb[hd] * lane)[0:1, :] for hd in range(N_HEADS)]
    c_decay = [jnp.exp(lgb[hd] * float(CHUNK))[0:1, :] for hd in range(N_HEADS)]

    def kv_step(i, carry):
        for u in range(RET_UNROLL):
            c = i * RET_UNROLL + u
            rows = pl.ds(pl.multiple_of(c * CHUNK, CHUNK), CHUNK)
            for hd in range(N_HEADS):
                cols = slice(hd * LANES, (hd + 1) * LANES)
                kd = (kt_ref[c, cols, :].astype(F32) * k_decay[hd]).astype(BF16)
                kv_ref[c, hd] = jnp.dot(kd, v_ref[rows, cols], preferred_element_type=F32)
        return carry

    lax.fori_loop(0, n_chunks // RET_UNROLL, kv_step, 0)

    def scan_step(i, carry):
        c = n_chunks - 1 - i
        for hd in range(N_HEADS):
            state = carry_ref[hd]
            sb_ref[c, hd] = state.astype(BF16)
            carry_ref[hd] = state * c_decay[hd] + kv_ref[c, hd]
        return carry

    lax.fori_loop(0, n_chunks, scan_step, 0)

    @pl.when(pl.program_id(0) == 0)
    def _():
        lgf = _log_sigmoid(decf_ref[...])
        lgb = _log_sigmoid(decb_ref[...])
        i = lax.broadcasted_iota(jnp.int32, (CHUNK, CHUNK), 0).astype(F32)
        j = lax.broadcasted_iota(jnp.int32, (CHUNK, CHUNK), 1).astype(F32)
        diff = i - j
        for hd in range(N_HEADS):
            lf = lgf[hd][0:1, :]
            lb = lgb[hd][0:1, :]
            fwd = jnp.where(diff >= 0, jnp.exp(lf * jnp.maximum(diff, 0.0)), 0.0)
            bwd = jnp.where(diff <= 0, jnp.exp(lb * jnp.maximum(-diff, 0.0)), 0.0)
            mask_ref[hd] = fwd + bwd
            qdf_ref[hd] = jnp.exp(lf * (i + 1.0)).astype(BF16)
            qdb_ref[hd] = jnp.exp(lb * (float(CHUNK) - i)).astype(BF16)
        gs_ref[...] = gain_ref[...] * (float(LANES) ** 0.5)

    def one_chunk(c):
        rows = pl.ds(pl.multiple_of(c * CHUNK, CHUNK), CHUNK)
        for hd in range(N_HEADS):
            cols = slice(hd * LANES, (hd + 1) * LANES)
            qh = q_ref[rows, cols]
            scores = jnp.dot(qh, kt_ref[c, cols, :], preferred_element_type=F32) * mask_ref[hd]
            lhs = jnp.concatenate([scores.astype(BF16), qh * qdf_ref[hd], qh * qdb_ref[hd]], axis=1)
            rhs = jnp.concatenate([v_ref[rows, cols], sf_ref[c, hd], sb_ref[c, hd]], axis=0)
            d = jnp.dot(lhs, rhs, preferred_element_type=F32)
            ss = jnp.sum(d * d, axis=-1, keepdims=True)
            yn = d * lax.rsqrt(ss + float(LANES) * EPS) * gs_ref[:, cols]
            gate = g_ref[rows, cols].astype(F32)
            o_ref[rows, cols] = (gate * jax.nn.sigmoid(gate) * yn).astype(BF16)

    def body(i, carry):
        for u in range(RET_UNROLL):
            one_chunk(i * RET_UNROLL + u)
        return carry

    lax.fori_loop(0, n_chunks // RET_UNROLL, body, 0)


def _ret_main(dec_f, dec_b, gain, q, kt, v, g, sf, *, seq, blk):
    t = v.shape[0]
    nb = t // blk
    n_chunks = blk // CHUNK
    assert n_chunks % RET_UNROLL == 0
    row = lambda j: (nb - 1 - j, 0)
    return pl.pallas_call(
        functools.partial(_ret_main_kernel, blocks_per_seq=seq // blk),
        out_shape=jax.ShapeDtypeStruct((t, D_HALF), BF16),
        grid=(nb,),
        in_specs=[
            _const_spec((N_HEADS, SUBLANES, LANES)),
            _const_spec((N_HEADS, SUBLANES, LANES)),
            _const_spec((1, D_HALF)),
            pl.BlockSpec((blk, D_HALF), row),
            pl.BlockSpec((n_chunks, D_HALF, CHUNK), lambda j: (nb - 1 - j, 0, 0)),
            pl.BlockSpec((blk, D_HALF), row),
            pl.BlockSpec((blk, D_HALF), row),
            pl.BlockSpec((n_chunks, N_HEADS, LANES, LANES), lambda j: (nb - 1 - j, 0, 0, 0)),
        ],
        out_specs=pl.BlockSpec((blk, D_HALF), row),
        scratch_shapes=[
            pltpu.VMEM((N_HEADS, CHUNK, CHUNK), F32),
            pltpu.VMEM((N_HEADS, CHUNK, CHUNK), BF16),
            pltpu.VMEM((N_HEADS, CHUNK, CHUNK), BF16),
            pltpu.VMEM((1, D_HALF), F32),
            pltpu.VMEM((n_chunks, N_HEADS, LANES, LANES), F32),
            pltpu.VMEM((n_chunks, N_HEADS, LANES, LANES), BF16),
            pltpu.VMEM((N_HEADS, LANES, LANES), F32),
        ],
        compiler_params=_params(1),
        name="ret_main",
    )(dec_f, dec_b, gain, q, kt, v, g, sf)


def _slab_rows(ref, j):
    slabs, blocks, _, tile, lanes = ref.shape
    return jnp.concatenate([ref[sl, :, j].reshape(blocks * tile, lanes) for sl in range(slabs)], axis=1)


def _dft_stage_a_kernel(u_ref, m_ref, g_ref, y_ref):
    n_s2 = u_ref.shape[2]
    r = u_ref.shape[1] * u_ref.shape[3]
    words = jnp.concatenate([_slab_rows(u_ref, jj) for jj in range(n_s2)], axis=0)
    lo, hi = _unpack_pair(words)
    u = jnp.concatenate([lo, hi], axis=1).astype(BF16)
    zz = [jnp.dot(u[:, g * LANES:(g + 1) * LANES], m_ref[g], preferred_element_type=F32) for g in range(N_HEADS)]
    zr = jnp.concatenate([z[:, :LANES] for z in zz], axis=1).astype(BF16)
    zi = jnp.concatenate([z[:, LANES:] for z in zz], axis=1).astype(BF16)
    for jj in range(n_s2):
        rows = slice(jj * r, (jj + 1) * r)
        rhs = jnp.concatenate([zr[rows], zi[rows]], axis=0)
        y = jnp.dot(g_ref[jj], rhs, preferred_element_type=F32)
        yw = _pack_pair(y[:r], y[r:])
        for sl in range(N_HEADS):
            _store_interleaved(y_ref.at[sl, 0], jj, yw[:, sl * LANES:(sl + 1) * LANES])


def _dft_stage_b_kernel(y_ref, h_ref, o_ref):
    for kk in range(y_ref.shape[2]):
        yr, yi = _unpack_pair(_slab_rows(y_ref, kk))
        rhs = jnp.concatenate([yr, yi], axis=0).astype(BF16)
        out = jnp.dot(h_ref[...], rhs, preferred_element_type=F32)
        for sl in range(N_HEADS):
            _store_interleaved(o_ref.at[sl, 0], kk, out[:, sl * LANES:(sl + 1) * LANES])


def _dft_tables(seq, batch):
    n2 = LANES
    n1 = seq // n2
    r = batch * n1
    idx = np.arange(2 * r)
    part, seq_id, digit = idx // r, (idx % r) // n1, idx % n1
    theta_a = 2.0 * np.pi * ((digit[:, None] * digit[None, :]) % n1) / n1
    theta_a = theta_a + (part[:, None] - part[None, :]) * (np.pi / 2.0)
    same_seq = seq_id[:, None] == seq_id[None, :]
    theta_b = 2.0 * np.pi * (np.arange(n2)[:, None] * digit[None, :]) / seq
    a_cos, a_sin = (jnp.asarray(f(theta_a) * same_seq, F32)[None] for f in (np.cos, np.sin))
    b_cos, b_sin = (jnp.asarray(f(theta_b), F32)[:, :, None] for f in (np.cos, np.sin))
    g = (a_cos * b_cos - a_sin * b_sin).astype(BF16)
    k2 = np.arange(n2)
    ang2 = 2.0 * np.pi * ((k2[:, None] * k2[None, :]) % n2) / n2
    scale = (seq * LANES) ** -0.5
    hm = jnp.asarray(np.concatenate([np.cos(ang2), np.sin(ang2)], axis=1) * scale, F32).astype(BF16)
    return g, hm


def _dft_two_stage(u, m, layer, g, hm):
    _, r_blocks, n2, tile, _ = u.shape
    r = r_blocks * tile
    y = pl.pallas_call(
        _dft_stage_a_kernel,
        out_shape=jax.ShapeDtypeStruct((N_HEADS, n2 // DFT_ROWS, r, DFT_ROWS, LANES), jnp.uint32),
        grid=(n2 // DFT_ROWS,),
        in_specs=[
            pl.BlockSpec((U_SLABS, r_blocks, DFT_ROWS, tile, LANES), lambda j: (0, 0, j, 0, 0)),
            _layer_spec(m.shape, layer),
            pl.BlockSpec((DFT_ROWS, 2 * r, 2 * r), lambda j: (j, 0, 0)),
        ],
        out_specs=pl.BlockSpec((N_HEADS, 1, r, DFT_ROWS, LANES), lambda j: (0, j, 0, 0, 0)),
        compiler_params=_params(1),
        name="dft_stage_a",
    )(u, m, g)
    return pl.pallas_call(
        _dft_stage_b_kernel,
        out_shape=jax.ShapeDtypeStruct((N_HEADS, r // DFT_ROWS, n2, DFT_ROWS, LANES), F32),
        grid=(r // DFT_ROWS,),
        in_specs=[
            pl.BlockSpec((N_HEADS, n2 // DFT_ROWS, DFT_ROWS, DFT_ROWS, LANES), lambda i: (0, 0, i, 0, 0)),
            _const_spec((n2, 2 * n2)),
        ],
        out_specs=pl.BlockSpec((N_HEADS, 1, n2, DFT_ROWS, LANES), lambda i: (0, i, 0, 0, 0)),
        compiler_params=_params(1),
        name="dft_stage_b",
    )(y, hm)


FF_CHUNK = 1024


def _out_mlp_kernel(x_ref, fo_ref, ro_ref, wo_ref, ln2_ref, w1_ref, w2_ref, lnf_ref, *rest, final, n_cast):
    o_ref = rest[n_cast]
    for src_ref, dst_ref in zip(rest[:n_cast], rest[n_cast + 1:]):
        dst_ref[...] = src_ref[...].astype(BF16)
    _, k1_blocks, n_k2, _, _ = fo_ref.shape
    fo = jnp.concatenate(
        [jnp.concatenate([fo_ref[sl, kb, k2] for k2 in range(n_k2) for kb in range(k1_blocks)], axis=0)
         for sl in range(N_HEADS)], axis=1).astype(BF16)
    x1 = x_ref[...] + jnp.dot(fo, wo_ref[:D_HALF], preferred_element_type=F32)
    x1 = x1 + jnp.dot(ro_ref[...], wo_ref[D_HALF:], preferred_element_type=F32)
    h = _rmsnorm(x1, ln2_ref[...]).astype(BF16)
    acc = None
    for c in range(w1_ref.shape[1] // FF_CHUNK):
        cols = slice(c * FF_CHUNK, (c + 1) * FF_CHUNK)
        a = jnp.dot(h, w1_ref[:, cols], preferred_element_type=F32)
        a = jnp.square(jnp.maximum(a, 0.0)).astype(BF16)
        part = jnp.dot(a, w2_ref[cols, :], preferred_element_type=F32)
        acc = part if acc is None else acc + part
    acc = acc + x1
    if final:
        acc = _rmsnorm(acc, lnf_ref[...])
    o_ref[...] = acc


def _out_mlp(x, fo, ro, weights, ln2, lnf, *, seq, final, cast_next=None):
    t, d = x.shape
    tm = 512
    steps = t // tm
    n1 = seq // LANES
    tiles_per_seq = seq // tm
    k1_blocks = n1 // DFT_ROWS
    fo = fo.reshape(N_HEADS, t // seq, k1_blocks, LANES, DFT_ROWS, LANES)
    row = lambda i: (i, 0)
    wo, w1, w2 = weights
    cast_args, cast_in_specs, cast_out_specs, cast_out_shapes = [], [], [], []
    if cast_next is not None:
        stacked, nxt = cast_next
        for w in stacked:
            rows, cols = w.shape[1] // steps, w.shape[2]
            cast_args.append(w)
            cast_in_specs.append(pl.BlockSpec((None, rows, cols), lambda i: (nxt, i, 0)))
            cast_out_specs.append(pl.BlockSpec((rows, cols), row))
            cast_out_shapes.append(jax.ShapeDtypeStruct(w.shape[1:], BF16))
    out = pl.pallas_call(
        functools.partial(_out_mlp_kernel, final=final, n_cast=len(cast_args)),
        out_shape=[jax.ShapeDtypeStruct((t, d), F32)] + cast_out_shapes,
        grid=(steps,),
        in_specs=[
            pl.BlockSpec((tm, d), row),
            pl.BlockSpec((N_HEADS, None, k1_blocks, tm // n1, DFT_ROWS, LANES),
                         lambda i: (0, i // tiles_per_seq, 0, i % tiles_per_seq, 0, 0)),
            pl.BlockSpec((tm, D_HALF), row),
            _const_spec(wo.shape),
            _const_spec((1, d)),
            _const_spec(w1.shape),
            _const_spec(w2.shape),
            _const_spec((1, d)),
        ] + cast_in_specs,
        out_specs=[pl.BlockSpec((tm, d), row)] + cast_out_specs,
        compiler_params=_params(1),
        name="out_mlp",
    )(x, fo, ro, wo, ln2, w1, w2, lnf, *cast_args)
    return out[0], tuple(out[1:])


def _rope_tables(seq):
    half = LANES // 2
    inv = ROPE_THETA ** (-np.arange(half, dtype=np.float64) / half)
    ang = np.arange(seq, dtype=np.float64)[:, None] * inv[None, :]
    c, s = np.cos(ang), np.sin(ang)
    cos_t = np.concatenate([c, c], axis=1).astype(np.float32)
    sin_t = np.concatenate([-s, s], axis=1).astype(np.float32)
    return jnp.asarray(cos_t), jnp.asarray(sin_t)


def _bcast_heads(v):
    return jnp.broadcast_to(v.astype(F32)[:, None, None], (N_HEADS, SUBLANES, LANES))


def _trunk(x, weights, mlp_weights, mlp_f32, *, seq):
    (wproj, m_dft), ln1, ln2, dec_f, dec_b, gain, lnf = weights
    depth = wproj.shape[0]
    cos_t, sin_t = _rope_tables(seq)
    g_dft, h_dft = _dft_tables(seq, x.shape[0] // seq)
    blk = min(seq, 2048)
    mlp_weights = list(mlp_weights)
    for l in range(depth):
        df, db = _bcast_heads(dec_f[l]), _bcast_heads(dec_b[l])
        u, q, kt, v, g, sf = _proj(x, ln1[l][None], wproj, l, cos_t, sin_t, df, seq=seq)
        ro = _ret_main(df, db, gain[l][None].astype(F32), q, kt, v, g, sf, seq=seq, blk=blk)
        fo = _dft_two_stage(u, m_dft, l, g_dft, h_dft)
        cast_next = (mlp_f32, l + 1) if len(mlp_weights) == l + 1 and l + 1 < depth else None
        x, nxt = _out_mlp(x, fo, ro, mlp_weights[l], ln2[l][None], lnf[None], seq=seq, final=(l == depth - 1),
                          cast_next=cast_next)
        if cast_next is not None:
            mlp_weights.append(nxt)
    return x, mlp_weights


def kernel(x_prompt, x_sample, ln1, w_in, w_fmix, decay_fwd, decay_bwd, gn_gain, w_o, ln2, w_ff1, w_ff2, ln_final):
    d = x_prompt.shape[-1]
    weights = (_prep_proj_weights(w_in, w_fmix), ln1, ln2, decay_fwd, decay_bwd, gn_gain, ln_final)
    mlp_f32 = (w_o, w_ff1, w_ff2)
    first = [tuple(w[0].astype(BF16) for w in mlp_f32)]
    y_prompt, mlp_weights = _trunk(x_prompt.reshape(-1, d), weights, first, mlp_f32, seq=x_prompt.shape[1])
    y_sample, _ = _trunk(x_sample.reshape(-1, d), weights, mlp_weights, mlp_f32, seq=x_sample.shape[1])
    return y_prompt.reshape(x_prompt.shape), y_sample.reshape(x_sample.shape)
```

```python
import functools

import jax
import jax.numpy as jnp
import numpy as np
from jax import lax
from jax.experimental import pallas as pl
from jax.experimental.pallas import tpu as pltpu

F32 = jnp.float32
BF16 = jnp.bfloat16

LANES = 128
SUBLANES = 8
N_HEADS = 4
D_HALF = N_HEADS * LANES
CHUNK = LANES
U_SLABS = D_HALF // (2 * LANES)
ROPE_THETA = 10000.0
EPS = 1e-6
VMEM_LIMIT = 56 * 1024 * 1024
DFT_ROWS = 8
DFT_GROUPS = 2


def _params(n_axes=1, vmem=VMEM_LIMIT):
    return pltpu.CompilerParams(dimension_semantics=("arbitrary",) * n_axes, vmem_limit_bytes=vmem)


def _store_interleaved(ref, a, val):
    rows, n, lanes = ref.shape
    ref.reshape(rows * n, lanes)[pl.ds(a, rows, stride=n), :] = val


def _const_spec(shape):
    zeros = (0,) * len(shape)
    return pl.BlockSpec(shape, lambda *_: zeros, pipeline_mode=pl.Buffered(1))


def _layer_spec(stacked_shape, layer):
    index = (layer,) + (0,) * (len(stacked_shape) - 1)
    return pl.BlockSpec((None,) + tuple(stacked_shape[1:]), lambda *_: index, pipeline_mode=pl.Buffered(1))


def _pack_pair(lo, hi):
    return pltpu.pack_elementwise([lo, hi], packed_dtype=BF16)


def _unpack_pair(words):
    lo = pltpu.unpack_elementwise(words, index=0, packed_dtype=BF16, unpacked_dtype=F32)
    hi = pltpu.unpack_elementwise(words, index=1, packed_dtype=BF16, unpacked_dtype=F32)
    return lo, hi


PROJ_U, PROJ_Q, PROJ_K, PROJ_V, PROJ_G = (b * D_HALF for b in range(5))


def _prep_proj_kernel(cs_ref, win_ref, wf_ref, o_ref, m_ref):
    n = pl.program_id(1)

    @pl.when(n == 0)
    def _():
        for g in range(N_HEADS):
            cw = jnp.dot(cs_ref[...], wf_ref[0, g], precision=lax.Precision.HIGHEST, preferred_element_type=F32)
            m_ref[0, g] = jnp.concatenate([cw[:LANES], cw[LANES:]], axis=1).astype(BF16)

    @pl.when(n == PROJ_V // D_HALF)
    def _():
        for g in range(N_HEADS):
            cols = slice(g * LANES, (g + 1) * LANES)
            w = win_ref[0, :, cols]
            o_ref[0, :, cols] = (w - jnp.mean(w, axis=-1, keepdims=True)).astype(BF16)

    @pl.when(n != PROJ_V // D_HALF)
    def _():
        o_ref[0] = win_ref[0].astype(BF16)


def _prep_proj_weights(w_in, w_fmix):
    depth, d, width = w_in.shape
    c = np.arange(LANES)
    ang = 2.0 * np.pi * ((c[:, None] * c[None, :]) % LANES) / LANES
    cs = jnp.asarray(np.concatenate([np.cos(ang), -np.sin(ang)], axis=0), F32)
    return pl.pallas_call(
        _prep_proj_kernel,
        out_shape=(
            jax.ShapeDtypeStruct((depth, d, width), BF16),
            jax.ShapeDtypeStruct((depth, N_HEADS, LANES, 2 * LANES), BF16),
        ),
        grid=(depth, width // D_HALF),
        in_specs=[
            pl.BlockSpec((2 * LANES, LANES), lambda l, n: (0, 0)),
            pl.BlockSpec((1, d, D_HALF), lambda l, n: (l, 0, n)),
            pl.BlockSpec((1, N_HEADS, LANES, LANES), lambda l, n: (l, 0, 0, 0)),
        ],
        out_specs=(
            pl.BlockSpec((1, d, D_HALF), lambda l, n: (l, 0, n)),
            pl.BlockSpec((1, N_HEADS, LANES, 2 * LANES), lambda l, n: (l, 0, 0, 0)),
        ),
        compiler_params=_params(2),
        name="prep_proj_weights",
    )(cs, w_in, w_fmix)


def _rmsnorm(x, gain):
    return x * lax.rsqrt(jnp.mean(x * x, axis=-1, keepdims=True) + EPS) * gain


def _log_sigmoid(x):
    return -(jnp.maximum(-x, 0.0) + jnp.log1p(jnp.exp(-jnp.abs(x))))


def _proj_kernel(x_ref, ln_ref, w_ref, cos_ref, sin_ref, decf_ref, u_ref, q_ref, kt_ref, v_ref, g_ref, sf_ref,
                 carry_ref, *, tiles_per_seq):
    tm = x_ref.shape[0]

    @pl.when(pl.program_id(0) % tiles_per_seq == 0)
    def _():
        carry_ref[...] = jnp.zeros_like(carry_ref)

    h = _rmsnorm(x_ref[...], ln_ref[...]).astype(BF16)

    def project(first_col):
        return jnp.dot(h, w_ref[:, first_col:first_col + D_HALF], preferred_element_type=F32)

    cos = cos_ref[...]
    sin = sin_ref[...]

    def rope(t):
        return t * cos + pltpu.roll(t, LANES // 2, 1) * sin

    kf = project(PROJ_K)
    v = project(PROJ_V).astype(BF16)
    v_ref[...] = v
    qf = project(PROJ_Q)
    k_scale = LANES ** -0.5
    lgf = _log_sigmoid(decf_ref[...])
    lane = lax.broadcasted_iota(jnp.int32, (SUBLANES, LANES), 1).astype(F32)
    for hd in range(N_HEADS):
        cols = slice(hd * LANES, (hd + 1) * LANES)
        q_ref[:, cols] = rope(qf[:, cols]).astype(BF16)
        kt = (rope(kf[:, cols]) * k_scale).T
        k_decay = jnp.exp(lgf[hd] * (float(CHUNK - 1) - lane))[0:1, :]
        c_decay = jnp.exp(lgf[hd] * float(CHUNK))[0:1, :]
        for c in range(tm // CHUNK):
            toks = slice(c * CHUNK, (c + 1) * CHUNK)
            kt_ref[c, cols, :] = kt[:, toks].astype(BF16)
            state = carry_ref[hd]
            sf_ref[c, hd] = state.astype(BF16)
            kv = jnp.dot((kt[:, toks] * k_decay).astype(BF16), v[toks, cols], preferred_element_type=F32)
            carry_ref[hd] = state * c_decay + kv

    u = project(PROJ_U)
    uw = _pack_pair(u[:, :D_HALF // 2], u[:, D_HALF // 2:])
    for sl in range(u_ref.shape[0]):
        for a in range(tm // LANES):
            _store_interleaved(u_ref.at[sl, 0], a, uw[a * LANES:(a + 1) * LANES, sl * LANES:(sl + 1) * LANES])
    g_ref[...] = project(PROJ_G).astype(BF16)


def _proj(x, ln, w, layer, cos_t, sin_t, dec_f, *, seq):
    t, d = x.shape
    tm = SUBLANES * LANES
    pos_blocks = seq // tm
    n_chunks = tm // CHUNK
    row = lambda i: (i, 0)
    return pl.pallas_call(
        functools.partial(_proj_kernel, tiles_per_seq=pos_blocks),
        out_shape=(
            jax.ShapeDtypeStruct((U_SLABS, t // tm, LANES, tm // LANES, LANES), jnp.uint32),
            jax.ShapeDtypeStruct((t, D_HALF), BF16),
            jax.ShapeDtypeStruct((t // CHUNK, D_HALF, CHUNK), BF16),
            jax.ShapeDtypeStruct((t, D_HALF), BF16),
            jax.ShapeDtypeStruct((t, D_HALF), BF16),
            jax.ShapeDtypeStruct((t // CHUNK, N_HEADS, LANES, LANES), BF16),
        ),
        grid=(t // tm,),
        in_specs=[
            pl.BlockSpec((tm, d), row),
            _const_spec((1, d)),
            _layer_spec(w.shape, layer),
            pl.BlockSpec((tm, LANES), lambda i: (i % pos_blocks, 0)),
            pl.BlockSpec((tm, LANES), lambda i: (i % pos_blocks, 0)),
            _const_spec((N_HEADS, SUBLANES, LANES)),
        ],
        out_specs=(
            pl.BlockSpec((U_SLABS, 1, LANES, tm // LANES, LANES), lambda i: (0, i, 0, 0, 0)),
            pl.BlockSpec((tm, D_HALF), row),
            pl.BlockSpec((n_chunks, D_HALF, CHUNK), lambda i: (i, 0, 0)),
            pl.BlockSpec((tm, D_HALF), row),
            pl.BlockSpec((tm, D_HALF), row),
            pl.BlockSpec((n_chunks, N_HEADS, LANES, LANES), lambda i: (i, 0, 0, 0)),
        ),
        scratch_shapes=[pltpu.VMEM((N_HEADS, LANES, LANES), F32)],
        compiler_params=_params(1),
        name="proj",
    )(x, ln, w, cos_t, sin_t, dec_f)


RET_UNROLL = 16


def _ret_main_kernel(decf_ref, decb_ref, gain_ref, q_ref, kt_ref, v_ref, g_ref, sf_ref, o_ref,
                     mask_ref, qdf_ref, qdb_ref, gs_ref, kv_ref, sb_ref, carry_ref, *, blocks_per_seq):
    n_chunks = kt_ref.shape[0]
    blk = pl.num_programs(0) - 1 - pl.program_id(0)

    @pl.when(blk % blocks_per_seq == blocks_per_seq - 1)
    def _():
        carry_ref[...] = jnp.zeros_like(carry_ref)

    lgb = _log_sigmoid(decb_ref[...])
    lane = lax.broadcasted_iota(jnp.int32, (SUBLANES, LANES), 1).astype(F32)
    k_decay = [jnp.exp(lgb[hd] * lane)[0:1, :] for hd in range(N_HEADS)]
    c_decay = [jnp.exp(lgb[hd] * float(CHUNK))[0:1, :] for hd in range(N_HEADS)]

    def kv_step(i, carry):
        for u in range(RET_UNROLL):
            c = i * RET_UNROLL + u
            rows = pl.ds(pl.multiple_of(c * CHUNK, CHUNK), CHUNK)
            for hd in range(N_HEADS):
                cols = slice(hd * LANES, (hd + 1) * LANES)
                kd = (kt_ref[c, cols, :].astype(F32) * k_decay[hd]).astype(BF16)
                kv_ref[c, hd] = jnp.dot(kd, v_ref[rows, cols], preferred_element_type=F32)
        return carry

    lax.fori_loop(0, n_chunks // RET_UNROLL, kv_step, 0)

    def scan_step(i, carry):
        c = n_chunks - 1 - i
        for hd in range(N_HEADS):
            state = carry_ref[hd]
            sb_ref[c, hd] = state.astype(BF16)
            carry_ref[hd] = state * c_decay[hd] + kv_ref[c, hd]
        return carry

    lax.fori_loop(0, n_chunks, scan_step, 0)

    @pl.when(pl.program_id(0) == 0)
    def _():
        lgf = _log_sigmoid(decf_ref[...])
        lgb = _log_sigmoid(decb_ref[...])
        i = lax.broadcasted_iota(jnp.int32, (CHUNK, CHUNK), 0).astype(F32)
        j = lax.broadcasted_iota(jnp.int32, (CHUNK, CHUNK), 1).astype(F32)
        diff = i - j
        for hd in range(N_HEADS):
            lf = lgf[hd][0:1, :]
            lb = lgb[hd][0:1, :]
            fwd = jnp.where(diff >= 0, jnp.exp(lf * jnp.maximum(diff, 0.0)), 0.0)
            bwd = jnp.where(diff <= 0, jnp.exp(lb * jnp.maximum(-diff, 0.0)), 0.0)
            mask_ref[hd] = fwd + bwd
            qdf_ref[hd] = jnp.exp(lf * (i + 1.0)).astype(BF16)
            qdb_ref[hd] = jnp.exp(lb * (float(CHUNK) - i)).astype(BF16)
        gs_ref[...] = gain_ref[...] * (float(LANES) ** 0.5)

    def one_chunk(c):
        rows = pl.ds(pl.multiple_of(c * CHUNK, CHUNK), CHUNK)
        for hd in range(N_HEADS):
            cols = slice(hd * LANES, (hd + 1) * LANES)
            qh = q_ref[rows, cols]
            scores = jnp.dot(qh, kt_ref[c, cols, :], preferred_element_type=F32) * mask_ref[hd]
            lhs = jnp.concatenate([scores.astype(BF16), qh * qdf_ref[hd], qh * qdb_ref[hd]], axis=1)
            rhs = jnp.concatenate([v_ref[rows, cols], sf_ref[c, hd], sb_ref[c, hd]], axis=0)
            d = jnp.dot(lhs, rhs, preferred_element_type=F32)
            ss = jnp.sum(d * d, axis=-1, keepdims=True)
            yn = d * lax.rsqrt(ss + float(LANES) * EPS) * gs_ref[:, cols]
            gate = g_ref[rows, cols].astype(F32)
            o_ref[rows, cols] = (gate * jax.nn.sigmoid(gate) * yn).astype(BF16)

    def body(i, carry):
        for u in range(RET_UNROLL):
            one_chunk(i * RET_UNROLL + u)
        return carry

    lax.fori_loop(0, n_chunks // RET_UNROLL, body, 0)


def _ret_main(dec_f, dec_b, gain, q, kt, v, g, sf, *, seq, blk):
    t = v.shape[0]
    nb = t // blk
    n_chunks = blk // CHUNK
    assert n_chunks % RET_UNROLL == 0
    row = lambda j: (nb - 1 - j, 0)
    return pl.pallas_call(
        functools.partial(_ret_main_kernel, blocks_per_seq=seq // blk),
        out_shape=jax.ShapeDtypeStruct((t, D_HALF), BF16),
        grid=(nb,),
        in_specs=[
            _const_spec((N_HEADS, SUBLANES, LANES)),
            _const_spec((N_HEADS, SUBLANES, LANES)),
            _const_spec((1, D_HALF)),
            pl.BlockSpec((blk, D_HALF), row),
            pl.BlockSpec((n_chunks, D_HALF, CHUNK), lambda j: (nb - 1 - j, 0, 0)),
            pl.BlockSpec((blk, D_HALF), row),
            pl.BlockSpec((blk, D_HALF), row),
            pl.BlockSpec((n_chunks, N_HEADS, LANES, LANES), lambda j: (nb - 1 - j, 0, 0, 0)),
        ],
        out_specs=pl.BlockSpec((blk, D_HALF), row),
        scratch_shapes=[
            pltpu.VMEM((N_HEADS, CHUNK, CHUNK), F32),
            pltpu.VMEM((N_HEADS, CHUNK, CHUNK), BF16),
            pltpu.VMEM((N_HEADS, CHUNK, CHUNK), BF16),
            pltpu.VMEM((1, D_HALF), F32),
            pltpu.VMEM((n_chunks, N_HEADS, LANES, LANES), F32),
            pltpu.VMEM((n_chunks, N_HEADS, LANES, LANES), BF16),
            pltpu.VMEM((N_HEADS, LANES, LANES), F32),
        ],
        compiler_params=_params(1),
        name="ret_main",
    )(dec_f, dec_b, gain, q, kt, v, g, sf)


def _slab_rows(ref, j):
    slabs, blocks, _, tile, lanes = ref.shape
    return jnp.concatenate([ref[sl, :, j].reshape(blocks * tile, lanes) for sl in range(slabs)], axis=1)


def _dft_stage_a_kernel(u_ref, m_ref, g_ref, y_ref):
    r = u_ref.shape[1] * u_ref.shape[3]
    for grp in range(y_ref.shape[1]):
        s2 = [grp * DFT_ROWS + jj for jj in range(DFT_ROWS)]
        words = jnp.concatenate([_slab_rows(u_ref, j) for j in s2], axis=0)
        lo, hi = _unpack_pair(words)
        u = jnp.concatenate([lo, hi], axis=1).astype(BF16)
        zz = [jnp.dot(u[:, g * LANES:(g + 1) * LANES], m_ref[g], preferred_element_type=F32) for g in range(N_HEADS)]
        zr = jnp.concatenate([z[:, :LANES] for z in zz], axis=1).astype(BF16)
        zi = jnp.concatenate([z[:, LANES:] for z in zz], axis=1).astype(BF16)
        for jj, j in enumerate(s2):
            rows = slice(jj * r, (jj + 1) * r)
            rhs = jnp.concatenate([zr[rows], zi[rows]], axis=0)
            y = jnp.dot(g_ref[j], rhs, preferred_element_type=F32)
            yw = _pack_pair(y[:r], y[r:])
            for sl in range(N_HEADS):
                _store_interleaved(y_ref.at[sl, grp], jj, yw[:, sl * LANES:(sl + 1) * LANES])


def _dft_stage_b_kernel(y_ref, h_ref, o_ref):
    for grp in range(o_ref.shape[1]):
        for kk in range(DFT_ROWS):
            yr, yi = _unpack_pair(_slab_rows(y_ref, grp * DFT_ROWS + kk))
            rhs = jnp.concatenate([yr, yi], axis=0).astype(BF16)
            out = jnp.dot(h_ref[...], rhs, preferred_element_type=F32)
            for sl in range(N_HEADS):
                _store_interleaved(o_ref.at[sl, grp], kk, out[:, sl * LANES:(sl + 1) * LANES])


def _dft_tables(seq, batch):
    n2 = LANES
    n1 = seq // n2
    r = batch * n1
    idx = np.arange(2 * r)
    part, seq_id, digit = idx // r, (idx % r) // n1, idx % n1
    theta_a = 2.0 * np.pi * ((digit[:, None] * digit[None, :]) % n1) / n1
    theta_a = theta_a + (part[:, None] - part[None, :]) * (np.pi / 2.0)
    same_seq = seq_id[:, None] == seq_id[None, :]
    theta_b = 2.0 * np.pi * (np.arange(n2)[:, None] * digit[None, :]) / seq
    a_cos, a_sin = (jnp.asarray(f(theta_a) * same_seq, F32)[None] for f in (np.cos, np.sin))
    b_cos, b_sin = (jnp.asarray(f(theta_b), F32)[:, :, None] for f in (np.cos, np.sin))
    g = (a_cos * b_cos - a_sin * b_sin).astype(BF16)
    k2 = np.arange(n2)
    ang2 = 2.0 * np.pi * ((k2[:, None] * k2[None, :]) % n2) / n2
    scale = (seq * LANES) ** -0.5
    hm = jnp.asarray(np.concatenate([np.cos(ang2), np.sin(ang2)], axis=1) * scale, F32).astype(BF16)
    return g, hm


def _dft_two_stage(u, m, layer, g, hm):
    _, r_blocks, n2, tile, _ = u.shape
    r = r_blocks * tile
    per_step = DFT_GROUPS * DFT_ROWS
    y = pl.pallas_call(
        _dft_stage_a_kernel,
        out_shape=jax.ShapeDtypeStruct((N_HEADS, n2 // DFT_ROWS, r, DFT_ROWS, LANES), jnp.uint32),
        grid=(n2 // per_step,),
        in_specs=[
            pl.BlockSpec((U_SLABS, r_blocks, per_step, tile, LANES), lambda j: (0, 0, j, 0, 0)),
            _layer_spec(m.shape, layer),
            pl.BlockSpec((per_step, 2 * r, 2 * r), lambda j: (j, 0, 0)),
        ],
        out_specs=pl.BlockSpec((N_HEADS, DFT_GROUPS, r, DFT_ROWS, LANES), lambda j: (0, j, 0, 0, 0)),
        compiler_params=_params(1),
        name="dft_stage_a",
    )(u, m, g)
    return pl.pallas_call(
        _dft_stage_b_kernel,
        out_shape=jax.ShapeDtypeStruct((N_HEADS, r // DFT_ROWS, n2, DFT_ROWS, LANES), F32),
        grid=(r // per_step,),
        in_specs=[
            pl.BlockSpec((N_HEADS, n2 // DFT_ROWS, per_step, DFT_ROWS, LANES), lambda i: (0, 0, i, 0, 0)),
            _const_spec((n2, 2 * n2)),
        ],
        out_specs=pl.BlockSpec((N_HEADS, DFT_GROUPS, n2, DFT_ROWS, LANES), lambda i: (0, i, 0, 0, 0)),
        compiler_params=_params(1),
        name="dft_stage_b",
    )(y, hm)


FF_CHUNK = 1024


def _out_mlp_kernel(x_ref, fo_ref, ro_ref, wo_ref, ln2_ref, w1_ref, w2_ref, lnf_ref, *rest, final, n_cast):
    o_ref = rest[n_cast]
    for src_ref, dst_ref in zip(rest[:n_cast], rest[n_cast + 1:]):
        dst_ref[...] = src_ref[...].astype(BF16)
    _, k1_blocks, n_k2, _, _ = fo_ref.shape
    fo = jnp.concatenate(
        [jnp.concatenate([fo_ref[sl, kb, k2] for k2 in range(n_k2) for kb in range(k1_blocks)], axis=0)
         for sl in range(N_HEADS)], axis=1).astype(BF16)
    x1 = x_ref[...] + jnp.dot(fo, wo_ref[:D_HALF], preferred_element_type=F32)
    x1 = x1 + jnp.dot(ro_ref[...], wo_ref[D_HALF:], preferred_element_type=F32)
    h = _rmsnorm(x1, ln2_ref[...]).astype(BF16)
    acc = None
    for c in range(w1_ref.shape[1] // FF_CHUNK):
        cols = slice(c * FF_CHUNK, (c + 1) * FF_CHUNK)
        a = jnp.dot(h, w1_ref[:, cols], preferred_element_type=F32)
        a = jnp.square(jnp.maximum(a, 0.0)).astype(BF16)
        part = jnp.dot(a, w2_ref[cols, :], preferred_element_type=F32)
        acc = part if acc is None else acc + part
    acc = acc + x1
    if final:
        acc = _rmsnorm(acc, lnf_ref[...])
    o_ref[...] = acc


def _out_mlp(x, fo, ro, weights, ln2, lnf, *, seq, final, cast_next=None):
    t, d = x.shape
    tm = 512
    steps = t // tm
    n1 = seq // LANES
    tiles_per_seq = seq // tm
    k1_blocks = n1 // DFT_ROWS
    fo = fo.reshape(N_HEADS, t // seq, k1_blocks, LANES, DFT_ROWS, LANES)
    row = lambda i: (i, 0)
    wo, w1, w2 = weights
    cast_args, cast_in_specs, cast_out_specs, cast_out_shapes = [], [], [], []
    if cast_next is not None:
        stacked, nxt = cast_next
        for w in stacked:
            rows, cols = w.shape[1] // steps, w.shape[2]
            cast_args.append(w)
            cast_in_specs.append(pl.BlockSpec((None, rows, cols), lambda i: (nxt, i, 0)))
            cast_out_specs.append(pl.BlockSpec((rows, cols), row))
            cast_out_shapes.append(jax.ShapeDtypeStruct(w.shape[1:], BF16))
    out = pl.pallas_call(
        functools.partial(_out_mlp_kernel, final=final, n_cast=len(cast_args)),
        out_shape=[jax.ShapeDtypeStruct((t, d), F32)] + cast_out_shapes,
        grid=(steps,),
        in_specs=[
            pl.BlockSpec((tm, d), row),
            pl.BlockSpec((N_HEADS, None, k1_blocks, tm // n1, DFT_ROWS, LANES),
                         lambda i: (0, i // tiles_per_seq, 0, i % tiles_per_seq, 0, 0)),
            pl.BlockSpec((tm, D_HALF), row),
            _const_spec(wo.shape),
            _const_spec((1, d)),
            _const_spec(w1.shape),
            _const_spec(w2.shape),
            _const_spec((1, d)),
        ] + cast_in_specs,
        out_specs=[pl.BlockSpec((tm, d), row)] + cast_out_specs,
        compiler_params=_params(1),
        name="out_mlp",
    )(x, fo, ro, wo, ln2, w1, w2, lnf, *cast_args)
    return out[0], tuple(out[1:])


def _rope_tables(seq):
    half = LANES // 2
    inv = ROPE_THETA ** (-np.arange(half, dtype=np.float64) / half)
    ang = np.arange(seq, dtype=np.float64)[:, None] * inv[None, :]
    c, s = np.cos(ang), np.sin(ang)
    cos_t = np.concatenate([c, c], axis=1).astype(np.float32)
    sin_t = np.concatenate([-s, s], axis=1).astype(np.float32)
    return jnp.asarray(cos_t), jnp.asarray(sin_t)


def _bcast_heads(v):
    return jnp.broadcast_to(v.astype(F32)[:, None, None], (N_HEADS, SUBLANES, LANES))


def _trunk(x, weights, mlp_weights, mlp_f32, *, seq):
    (wproj, m_dft), ln1, ln2, dec_f, dec_b, gain, lnf = weights
    depth = wproj.shape[0]
    cos_t, sin_t = _rope_tables(seq)
    g_dft, h_dft = _dft_tables(seq, x.shape[0] // seq)
    blk = min(seq, 2048)
    mlp_weights = list(mlp_weights)
    for l in range(depth):
        df, db = _bcast_heads(dec_f[l]), _bcast_heads(dec_b[l])
        u, q, kt, v, g, sf = _proj(x, ln1[l][None], wproj, l, cos_t, sin_t, df, seq=seq)
        ro = _ret_main(df, db, gain[l][None].astype(F32), q, kt, v, g, sf, seq=seq, blk=blk)
        fo = _dft_two_stage(u, m_dft, l, g_dft, h_dft)
        cast_next = (mlp_f32, l + 1) if len(mlp_weights) == l + 1 and l + 1 < depth else None
        x, nxt = _out_mlp(x, fo, ro, mlp_weights[l], ln2[l][None], lnf[None], seq=seq, final=(l == depth - 1),
                          cast_next=cast_next)
        if cast_next is not None:
            mlp_weights.append(nxt)
    return x, mlp_weights


def kernel(x_prompt, x_sample, ln1, w_in, w_fmix, decay_fwd, decay_bwd, gn_gain, w_o, ln2, w_ff1, w_ff2, ln_final):
    d = x_prompt.shape[-1]
    weights = (_prep_proj_weights(w_in, w_fmix), ln1, ln2, decay_fwd, decay_bwd, gn_gain, ln_final)
    mlp_f32 = (w_o, w_ff1, w_ff2)
    first = [tuple(w[0].astype(BF16) for w in mlp_f32)]
    y_prompt, mlp_weights = _trunk(x_prompt.reshape(-1, d), weights, first, mlp_f32, seq=x_prompt.shape[1])
    y_sample, _ = _trunk(x_sample.reshape(-1, d), weights, mlp_weights, mlp_f32, seq=x_sample.shape[1])
    return y_prompt.reshape(x_prompt.shape), y_sample.reshape(x_sample.shape)
```

```python
import functools

import jax
import jax.numpy as jnp
import numpy as np
from jax import lax
from jax.experimental import pallas as pl
from jax.experimental.pallas import tpu as pltpu

F32 = jnp.float32
BF16 = jnp.bfloat16

LANES = 128
SUBLANES = 8
N_HEADS = 4
D_HALF = N_HEADS * LANES
CHUNK = LANES
U_SLABS = D_HALF // (2 * LANES)
ROPE_THETA = 10000.0
EPS = 1e-6
VMEM_LIMIT = 56 * 1024 * 1024
DFT_ROWS = 8
DFT_GROUPS = 4


def _params(n_axes=1, vmem=VMEM_LIMIT):
    return pltpu.CompilerParams(dimension_semantics=("arbitrary",) * n_axes, vmem_limit_bytes=vmem)


def _store_interleaved(ref, a, val):
    rows, n, lanes = ref.shape
    ref.reshape(rows * n, lanes)[pl.ds(a, rows, stride=n), :] = val


def _const_spec(shape):
    zeros = (0,) * len(shape)
    return pl.BlockSpec(shape, lambda *_: zeros, pipeline_mode=pl.Buffered(1))


def _layer_spec(stacked_shape, layer):
    index = (layer,) + (0,) * (len(stacked_shape) - 1)
    return pl.BlockSpec((None,) + tuple(stacked_shape[1:]), lambda *_: index, pipeline_mode=pl.Buffered(1))


def _pack_pair(lo, hi):
    return pltpu.pack_elementwise([lo, hi], packed_dtype=BF16)


def _unpack_pair(words):
    lo = pltpu.unpack_elementwise(words, index=0, packed_dtype=BF16, unpacked_dtype=F32)
    hi = pltpu.unpack_elementwise(words, index=1, packed_dtype=BF16, unpacked_dtype=F32)
    return lo, hi


PROJ_U, PROJ_Q, PROJ_K, PROJ_V, PROJ_G = (b * D_HALF for b in range(5))


def _prep_proj_kernel(cs_ref, win_ref, wf_ref, o_ref, m_ref):
    n = pl.program_id(1)

    @pl.when(n == 0)
    def _():
        for g in range(N_HEADS):
            cw = jnp.dot(cs_ref[...], wf_ref[0, g], precision=lax.Precision.HIGHEST, preferred_element_type=F32)
            m_ref[0, g] = jnp.concatenate([cw[:LANES], cw[LANES:]], axis=1).astype(BF16)

    @pl.when(n == PROJ_V // D_HALF)
    def _():
        for g in range(N_HEADS):
            cols = slice(g * LANES, (g + 1) * LANES)
            w = win_ref[0, :, cols]
            o_ref[0, :, cols] = (w - jnp.mean(w, axis=-1, keepdims=True)).astype(BF16)

    @pl.when(n != PROJ_V // D_HALF)
    def _():
        o_ref[0] = win_ref[0].astype(BF16)


def _prep_proj_weights(w_in, w_fmix):
    depth, d, width = w_in.shape
    c = np.arange(LANES)
    ang = 2.0 * np.pi * ((c[:, None] * c[None, :]) % LANES) / LANES
    cs = jnp.asarray(np.concatenate([np.cos(ang), -np.sin(ang)], axis=0), F32)
    return pl.pallas_call(
        _prep_proj_kernel,
        out_shape=(
            jax.ShapeDtypeStruct((depth, d, width), BF16),
            jax.ShapeDtypeStruct((depth, N_HEADS, LANES, 2 * LANES), BF16),
        ),
        grid=(depth, width // D_HALF),
        in_specs=[
            pl.BlockSpec((2 * LANES, LANES), lambda l, n: (0, 0)),
            pl.BlockSpec((1, d, D_HALF), lambda l, n: (l, 0, n)),
            pl.BlockSpec((1, N_HEADS, LANES, LANES), lambda l, n: (l, 0, 0, 0)),
        ],
        out_specs=(
            pl.BlockSpec((1, d, D_HALF), lambda l, n: (l, 0, n)),
            pl.BlockSpec((1, N_HEADS, LANES, 2 * LANES), lambda l, n: (l, 0, 0, 0)),
        ),
        compiler_params=_params(2),
        name="prep_proj_weights",
    )(cs, w_in, w_fmix)


def _rmsnorm(x, gain):
    return x * lax.rsqrt(jnp.mean(x * x, axis=-1, keepdims=True) + EPS) * gain


def _log_sigmoid(x):
    return -(jnp.maximum(-x, 0.0) + jnp.log1p(jnp.exp(-jnp.abs(x))))


def _proj_kernel(x_ref, ln_ref, w_ref, cos_ref, sin_ref, decf_ref, u_ref, q_ref, kt_ref, v_ref, g_ref, sf_ref,
                 carry_ref, *, tiles_per_seq):
    tm = x_ref.shape[0]

    @pl.when(pl.program_id(0) % tiles_per_seq == 0)
    def _():
        carry_ref[...] = jnp.zeros_like(carry_ref)

    h = _rmsnorm(x_ref[...], ln_ref[...]).astype(BF16)

    def project(first_col):
        return jnp.dot(h, w_ref[:, first_col:first_col + D_HALF], preferred_element_type=F32)

    cos = cos_ref[...]
    sin = sin_ref[...]

    def rope(t):
        return t * cos + pltpu.roll(t, LANES // 2, 1) * sin

    kf = project(PROJ_K)
    v = project(PROJ_V).astype(BF16)
    v_ref[...] = v
    qf = project(PROJ_Q)
    k_scale = LANES ** -0.5
    lgf = _log_sigmoid(decf_ref[...])
    lane = lax.broadcasted_iota(jnp.int32, (SUBLANES, LANES), 1).astype(F32)
    for hd in range(N_HEADS):
        cols = slice(hd * LANES, (hd + 1) * LANES)
        q_ref[:, cols] = rope(qf[:, cols]).astype(BF16)
        kt = (rope(kf[:, cols]) * k_scale).T
        k_decay = jnp.exp(lgf[hd] * (float(CHUNK - 1) - lane))[0:1, :]
        c_decay = jnp.exp(lgf[hd] * float(CHUNK))[0:1, :]
        for c in range(tm // CHUNK):
            toks = slice(c * CHUNK, (c + 1) * CHUNK)
            kt_ref[c, cols, :] = kt[:, toks].astype(BF16)
            state = carry_ref[hd]
            sf_ref[c, hd] = state.astype(BF16)
            kv = jnp.dot((kt[:, toks] * k_decay).astype(BF16), v[toks, cols], preferred_element_type=F32)
            carry_ref[hd] = state * c_decay + kv

    u = project(PROJ_U)
    uw = _pack_pair(u[:, :D_HALF // 2], u[:, D_HALF // 2:])
    for sl in range(u_ref.shape[0]):
        for a in range(tm // LANES):
            _store_interleaved(u_ref.at[sl, 0], a, uw[a * LANES:(a + 1) * LANES, sl * LANES:(sl + 1) * LANES])
    g_ref[...] = project(PROJ_G).astype(BF16)


def _proj(x, ln, w, layer, cos_t, sin_t, dec_f, *, seq):
    t, d = x.shape
    tm = SUBLANES * LANES
    pos_blocks = seq // tm
    n_chunks = tm // CHUNK
    row = lambda i: (i, 0)
    return pl.pallas_call(
        functools.partial(_proj_kernel, tiles_per_seq=pos_blocks),
        out_shape=(
            jax.ShapeDtypeStruct((U_SLABS, t // tm, LANES, tm // LANES, LANES), jnp.uint32),
            jax.ShapeDtypeStruct((t, D_HALF), BF16),
            jax.ShapeDtypeStruct((t // CHUNK, D_HALF, CHUNK), BF16),
            jax.ShapeDtypeStruct((t, D_HALF), BF16),
            jax.ShapeDtypeStruct((t, D_HALF), BF16),
            jax.ShapeDtypeStruct((t // CHUNK, N_HEADS, LANES, LANES), BF16),
        ),
        grid=(t // tm,),
        in_specs=[
            pl.BlockSpec((tm, d), row),
            _const_spec((1, d)),
            _layer_spec(w.shape, layer),
            pl.BlockSpec((tm, LANES), lambda i: (i % pos_blocks, 0)),
            pl.BlockSpec((tm, LANES), lambda i: (i % pos_blocks, 0)),
            _const_spec((N_HEADS, SUBLANES, LANES)),
        ],
        out_specs=(
            pl.BlockSpec((U_SLABS, 1, LANES, tm // LANES, LANES), lambda i: (0, i, 0, 0, 0)),
            pl.BlockSpec((tm, D_HALF), row),
            pl.BlockSpec((n_chunks, D_HALF, CHUNK), lambda i: (i, 0, 0)),
            pl.BlockSpec((tm, D_HALF), row),
            pl.BlockSpec((tm, D_HALF), row),
            pl.BlockSpec((n_chunks, N_HEADS, LANES, LANES), lambda i: (i, 0, 0, 0)),
        ),
        scratch_shapes=[pltpu.VMEM((N_HEADS, LANES, LANES), F32)],
        compiler_params=_params(1),
        name="proj",
    )(x, ln, w, cos_t, sin_t, dec_f)


RET_UNROLL = 16


def _ret_main_kernel(decf_ref, decb_ref, gain_ref, q_ref, kt_ref, v_ref, g_ref, sf_ref, o_ref,
                     mask_ref, qdf_ref, qdb_ref, gs_ref, kv_ref, sb_ref, carry_ref, *, blocks_per_seq):
    n_chunks = kt_ref.shape[0]
    blk = pl.num_programs(0) - 1 - pl.program_id(0)

    @pl.when(blk % blocks_per_seq == blocks_per_seq - 1)
    def _():
        carry_ref[...] = jnp.zeros_like(carry_ref)

    lgb = _log_sigmoid(decb_ref[...])
    lane = lax.broadcasted_iota(jnp.int32, (SUBLANES, LANES), 1).astype(F32)
    k_decay = [jnp.exp(lgb[hd] * lane)[0:1, :] for hd in range(N_HEADS)]
    c_decay = [jnp.exp(lgb[hd] * float(CHUNK))[0:1, :] for hd in range(N_HEADS)]

    def kv_step(i, carry):
        for u in range(RET_UNROLL):
            c = i * RET_UNROLL + u
            rows = pl.ds(pl.multiple_of(c * CHUNK, CHUNK), CHUNK)
            for hd in range(N_HEADS):
                cols = slice(hd * LANES, (hd + 1) * LANES)
                kd = (kt_ref[c, cols, :].astype(F32) * k_decay[hd]).astype(BF16)
                kv_ref[c, hd] = jnp.dot(kd, v_ref[rows, cols], preferred_element_type=F32)
        return carry

    lax.fori_loop(0, n_chunks // RET_UNROLL, kv_step, 0)

    def scan_step(i, carry):
        c = n_chunks - 1 - i
        for hd in range(N_HEADS):
            state = carry_ref[hd]
            sb_ref[c, hd] = state.astype(BF16)
            carry_ref[hd] = state * c_decay[hd] + kv_ref[c, hd]
        return carry

    lax.fori_loop(0, n_chunks, scan_step, 0)

    @pl.when(pl.program_id(0) == 0)
    def _():
        lgf = _log_sigmoid(decf_ref[...])
        lgb = _log_sigmoid(decb_ref[...])
        i = lax.broadcasted_iota(jnp.int32, (CHUNK, CHUNK), 0).astype(F32)
        j = lax.broadcasted_iota(jnp.int32, (CHUNK, CHUNK), 1).astype(F32)
        diff = i - j
        for hd in range(N_HEADS):
            lf = lgf[hd][0:1, :]
            lb = lgb[hd][0:1, :]
            fwd = jnp.where(diff >= 0, jnp.exp(lf * jnp.maximum(diff, 0.0)), 0.0)
            bwd = jnp.where(diff <= 0, jnp.exp(lb * jnp.maximum(-diff, 0.0)), 0.0)
            mask_ref[hd] = fwd + bwd
            qdf_ref[hd] = jnp.exp(lf * (i + 1.0)).astype(BF16)
            qdb_ref[hd] = jnp.exp(lb * (float(CHUNK) - i)).astype(BF16)
        gs_ref[...] = gain_ref[...] * (float(LANES) ** 0.5)

    def one_chunk(c):
        rows = pl.ds(pl.multiple_of(c * CHUNK, CHUNK), CHUNK)
        for hd in range(N_HEADS):
            cols = slice(hd * LANES, (hd + 1) * LANES)
            qh = q_ref[rows, cols]
            scores = jnp.dot(qh, kt_ref[c, cols, :], preferred_element_type=F32) * mask_ref[hd]
            lhs = jnp.concatenate([scores.astype(BF16), qh * qdf_ref[hd], qh * qdb_ref[hd]], axis=1)
            rhs = jnp.concatenate([v_ref[rows, cols], sf_ref[c, hd], sb_ref[c, hd]], axis=0)
            d = jnp.dot(lhs, rhs, preferred_element_type=F32)
            ss = jnp.sum(d * d, axis=-1, keepdims=True)
            yn = d * lax.rsqrt(ss + float(LANES) * EPS) * gs_ref[:, cols]
            gate = g_ref[rows, cols].astype(F32)
            o_ref[rows, cols] = (gate * jax.nn.sigmoid(gate) * yn).astype(BF16)

    def body(i, carry):
        for u in range(RET_UNROLL):
            one_chunk(i * RET_UNROLL + u)
        return carry

    lax.fori_loop(0, n_chunks // RET_UNROLL, body, 0)


def _ret_main(dec_f, dec_b, gain, q, kt, v, g, sf, *, seq, blk):
    t = v.shape[0]
    nb = t // blk
    n_chunks = blk // CHUNK
    assert n_chunks % RET_UNROLL == 0
    row = lambda j: (nb - 1 - j, 0)
    return pl.pallas_call(
        functools.partial(_ret_main_kernel, blocks_per_seq=seq // blk),
        out_shape=jax.ShapeDtypeStruct((t, D_HALF), BF16),
        grid=(nb,),
        in_specs=[
            _const_spec((N_HEADS, SUBLANES, LANES)),
            _const_spec((N_HEADS, SUBLANES, LANES)),
            _const_spec((1, D_HALF)),
            pl.BlockSpec((blk, D_HALF), row),
            pl.BlockSpec((n_chunks, D_HALF, CHUNK), lambda j: (nb - 1 - j, 0, 0)),
            pl.BlockSpec((blk, D_HALF), row),
            pl.BlockSpec((blk, D_HALF), row),
            pl.BlockSpec((n_chunks, N_HEADS, LANES, LANES), lambda j: (nb - 1 - j, 0, 0, 0)),
        ],
        out_specs=pl.BlockSpec((blk, D_HALF), row),
        scratch_shapes=[
            pltpu.VMEM((N_HEADS, CHUNK, CHUNK), F32),
            pltpu.VMEM((N_HEADS, CHUNK, CHUNK), BF16),
            pltpu.VMEM((N_HEADS, CHUNK, CHUNK), BF16),
            pltpu.VMEM((1, D_HALF), F32),
            pltpu.VMEM((n_chunks, N_HEADS, LANES, LANES), F32),
            pltpu.VMEM((n_chunks, N_HEADS, LANES, LANES), BF16),
            pltpu.VMEM((N_HEADS, LANES, LANES), F32),
        ],
        compiler_params=_params(1),
        name="ret_main",
    )(dec_f, dec_b, gain, q, kt, v, g, sf)


def _slab_rows(ref, j):
    slabs, blocks, _, tile, lanes = ref.shape
    return jnp.concatenate([ref[sl, :, j].reshape(blocks * tile, lanes) for sl in range(slabs)], axis=1)


def _dft_stage_a_kernel(u_ref, m_ref, g_ref, y_ref):
    r = u_ref.shape[1] * u_ref.shape[3]
    for grp in range(y_ref.shape[1]):
        s2 = [grp * DFT_ROWS + jj for jj in range(DFT_ROWS)]
        words = jnp.concatenate([_slab_rows(u_ref, j) for j in s2], axis=0)
        lo, hi = _unpack_pair(words)
        u = jnp.concatenate([lo, hi], axis=1).astype(BF16)
        zz = [jnp.dot(u[:, g * LANES:(g + 1) * LANES], m_ref[g], preferred_element_type=F32) for g in range(N_HEADS)]
        zr = jnp.concatenate([z[:, :LANES] for z in zz], axis=1).astype(BF16)
        zi = jnp.concatenate([z[:, LANES:] for z in zz], axis=1).astype(BF16)
        for jj, j in enumerate(s2):
            rows = slice(jj * r, (jj + 1) * r)
            rhs = jnp.concatenate([zr[rows], zi[rows]], axis=0)
            y = jnp.dot(g_ref[j], rhs, preferred_element_type=F32)
            yw = _pack_pair(y[:r], y[r:])
            for sl in range(N_HEADS):
                _store_interleaved(y_ref.at[sl, grp], jj, yw[:, sl * LANES:(sl + 1) * LANES])


def _dft_stage_b_kernel(y_ref, h_ref, o_ref):
    for grp in range(o_ref.shape[1]):
        for kk in range(DFT_ROWS):
            yr, yi = _unpack_pair(_slab_rows(y_ref, grp * DFT_ROWS + kk))
            rhs = jnp.concatenate([yr, yi], axis=0).astype(BF16)
            out = jnp.dot(h_ref[...], rhs, preferred_element_type=F32)
            for sl in range(N_HEADS):
                _store_interleaved(o_ref.at[sl, grp], kk, out[:, sl * LANES:(sl + 1) * LANES])


def _dft_tables(seq, batch):
    n2 = LANES
    n1 = seq // n2
    r = batch * n1
    idx = np.arange(2 * r)
    part, seq_id, digit = idx // r, (idx % r) // n1, idx % n1
    theta_a = 2.0 * np.pi * ((digit[:, None] * digit[None, :]) % n1) / n1
    theta_a = theta_a + (part[:, None] - part[None, :]) * (np.pi / 2.0)
    same_seq = seq_id[:, None] == seq_id[None, :]
    theta_b = 2.0 * np.pi * (np.arange(n2)[:, None] * digit[None, :]) / seq
    a_cos, a_sin = (jnp.asarray(f(theta_a) * same_seq, F32)[None] for f in (np.cos, np.sin))
    b_cos, b_sin = (jnp.asarray(f(theta_b), F32)[:, :, None] for f in (np.cos, np.sin))
    g = (a_cos * b_cos - a_sin * b_sin).astype(BF16)
    k2 = np.arange(n2)
    ang2 = 2.0 * np.pi * ((k2[:, None] * k2[None, :]) % n2) / n2
    scale = (seq * LANES) ** -0.5
    hm = jnp.asarray(np.concatenate([np.cos(ang2), np.sin(ang2)], axis=1) * scale, F32).astype(BF16)
    return g, hm


def _dft_two_stage(u, m, layer, g, hm):
    _, r_blocks, n2, tile, _ = u.shape
    r = r_blocks * tile
    per_step = DFT_GROUPS * DFT_ROWS
    y = pl.pallas_call(
        _dft_stage_a_kernel,
        out_shape=jax.ShapeDtypeStruct((N_HEADS, n2 // DFT_ROWS, r, DFT_ROWS, LANES), jnp.uint32),
        grid=(n2 // per_step,),
        in_specs=[
            pl.BlockSpec((U_SLABS, r_blocks, per_step, tile, LANES), lambda j: (0, 0, j, 0, 0)),
            _layer_spec(m.shape, layer),
            pl.BlockSpec((per_step, 2 * r, 2 * r), lambda j: (j, 0, 0)),
        ],
        out_specs=pl.BlockSpec((N_HEADS, DFT_GROUPS, r, DFT_ROWS, LANES), lambda j: (0, j, 0, 0, 0)),
        compiler_params=_params(1),
        name="dft_stage_a",
    )(u, m, g)
    return pl.pallas_call(
        _dft_stage_b_kernel,
        out_shape=jax.ShapeDtypeStruct((N_HEADS, r // DFT_ROWS, n2, DFT_ROWS, LANES), F32),
        grid=(r // per_step,),
        in_specs=[
            pl.BlockSpec((N_HEADS, n2 // DFT_ROWS, per_step, DFT_ROWS, LANES), lambda i: (0, 0, i, 0, 0)),
            _const_spec((n2, 2 * n2)),
        ],
        out_specs=pl.BlockSpec((N_HEADS, DFT_GROUPS, n2, DFT_ROWS, LANES), lambda i: (0, i, 0, 0, 0)),
        compiler_params=_params(1),
        name="dft_stage_b",
    )(y, hm)


FF_CHUNK = 1024


def _out_mlp_kernel(x_ref, fo_ref, ro_ref, wo_ref, ln2_ref, w1_ref, w2_ref, lnf_ref, *rest, final, n_cast):
    o_ref = rest[n_cast]
    for src_ref, dst_ref in zip(rest[:n_cast], rest[n_cast + 1:]):
        dst_ref[...] = src_ref[...].astype(BF16)
    _, k1_blocks, n_k2, _, _ = fo_ref.shape
    fo = jnp.concatenate(
        [jnp.concatenate([fo_ref[sl, kb, k2] for k2 in range(n_k2) for kb in range(k1_blocks)], axis=0)
         for sl in range(N_HEADS)], axis=1).astype(BF16)
    x1 = x_ref[...] + jnp.dot(fo, wo_ref[:D_HALF], preferred_element_type=F32)
    x1 = x1 + jnp.dot(ro_ref[...], wo_ref[D_HALF:], preferred_element_type=F32)
    h = _rmsnorm(x1, ln2_ref[...]).astype(BF16)
    acc = None
    for c in range(w1_ref.shape[1] // FF_CHUNK):
        cols = slice(c * FF_CHUNK, (c + 1) * FF_CHUNK)
        a = jnp.dot(h, w1_ref[:, cols], preferred_element_type=F32)
        a = jnp.square(jnp.maximum(a, 0.0)).astype(BF16)
        part = jnp.dot(a, w2_ref[cols, :], preferred_element_type=F32)
        acc = part if acc is None else acc + part
    acc = acc + x1
    if final:
        acc = _rmsnorm(acc, lnf_ref[...])
    o_ref[...] = acc


def _out_mlp(x, fo, ro, weights, ln2, lnf, *, seq, final, cast_next=None):
    t, d = x.shape
    tm = 512
    steps = t // tm
    n1 = seq // LANES
    tiles_per_seq = seq // tm
    k1_blocks = n1 // DFT_ROWS
    fo = fo.reshape(N_HEADS, t // seq, k1_blocks, LANES, DFT_ROWS, LANES)
    row = lambda i: (i, 0)
    wo, w1, w2 = weights
    cast_args, cast_in_specs, cast_out_specs, cast_out_shapes = [], [], [], []
    if cast_next is not None:
        stacked, nxt = cast_next
        for w in stacked:
            rows, cols = w.shape[1] // steps, w.shape[2]
            cast_args.append(w)
            cast_in_specs.append(pl.BlockSpec((None, rows, cols), lambda i: (nxt, i, 0)))
            cast_out_specs.append(pl.BlockSpec((rows, cols), row))
            cast_out_shapes.append(jax.ShapeDtypeStruct(w.shape[1:], BF16))
    out = pl.pallas_call(
        functools.partial(_out_mlp_kernel, final=final, n_cast=len(cast_args)),
        out_shape=[jax.ShapeDtypeStruct((t, d), F32)] + cast_out_shapes,
        grid=(steps,),
        in_specs=[
            pl.BlockSpec((tm, d), row),
            pl.BlockSpec((N_HEADS, None, k1_blocks, tm // n1, DFT_ROWS, LANES),
                         lambda i: (0, i // tiles_per_seq, 0, i % tiles_per_seq, 0, 0)),
            pl.BlockSpec((tm, D_HALF), row),
            _const_spec(wo.shape),
            _const_spec((1, d)),
            _const_spec(w1.shape),
            _const_spec(w2.shape),
            _const_spec((1, d)),
        ] + cast_in_specs,
        out_specs=[pl.BlockSpec((tm, d), row)] + cast_out_specs,
        compiler_params=_params(1),
        name="out_mlp",
    )(x, fo, ro, wo, ln2, w1, w2, lnf, *cast_args)
    return out[0], tuple(out[1:])


def _rope_tables(seq):
    half = LANES // 2
    inv = ROPE_THETA ** (-np.arange(half, dtype=np.float64) / half)
    ang = np.arange(seq, dtype=np.float64)[:, None] * inv[None, :]
    c, s = np.cos(ang), np.sin(ang)
    cos_t = np.concatenate([c, c], axis=1).astype(np.float32)
    sin_t = np.concatenate([-s, s], axis=1).astype(np.float32)
    return jnp.asarray(cos_t), jnp.asarray(sin_t)


def _bcast_heads(v):
    return jnp.broadcast_to(v.astype(F32)[:, None, None], (N_HEADS, SUBLANES, LANES))


def _trunk(x, weights, mlp_weights, mlp_f32, *, seq):
    (wproj, m_dft), ln1, ln2, dec_f, dec_b, gain, lnf = weights
    depth = wproj.shape[0]
    cos_t, sin_t = _rope_tables(seq)
    g_dft, h_dft = _dft_tables(seq, x.shape[0] // seq)
    blk = min(seq, 2048)
    mlp_weights = list(mlp_weights)
    for l in range(depth):
        df, db = _bcast_heads(dec_f[l]), _bcast_heads(dec_b[l])
        u, q, kt, v, g, sf = _proj(x, ln1[l][None], wproj, l, cos_t, sin_t, df, seq=seq)
        ro = _ret_main(df, db, gain[l][None].astype(F32), q, kt, v, g, sf, seq=seq, blk=blk)
        fo = _dft_two_stage(u, m_dft, l, g_dft, h_dft)
        cast_next = (mlp_f32, l + 1) if len(mlp_weights) == l + 1 and l + 1 < depth else None
        x, nxt = _out_mlp(x, fo, ro, mlp_weights[l], ln2[l][None], lnf[None], seq=seq, final=(l == depth - 1),
                          cast_next=cast_next)
        if cast_next is not None:
            mlp_weights.append(nxt)
    return x, mlp_weights


def kernel(x_prompt, x_sample, ln1, w_in, w_fmix, decay_fwd, decay_bwd, gn_gain, w_o, ln2, w_ff1, w_ff2, ln_final):
    d = x_prompt.shape[-1]
    weights = (_prep_proj_weights(w_in, w_fmix), ln1, ln2, decay_fwd, decay_bwd, gn_gain, ln_final)
    mlp_f32 = (w_o, w_ff1, w_ff2)
    first = [tuple(w[0].astype(BF16) for w in mlp_f32)]
    y_prompt, mlp_weights = _trunk(x_prompt.reshape(-1, d), weights, first, mlp_f32, seq=x_prompt.shape[1])
    y_sample, _ = _trunk(x_sample.reshape(-1, d), weights, mlp_weights, mlp_f32, seq=x_sample.shape[1])
    return y_prompt.reshape(x_prompt.shape), y_sample.reshape(x_sample.shape)
```

```python
import functools

import jax
import jax.numpy as jnp
import numpy as np
from jax import lax
from jax.experimental import pallas as pl
from jax.experimental.pallas import tpu as pltpu

F32 = jnp.float32
BF16 = jnp.bfloat16

LANES = 128
SUBLANES = 8
N_HEADS = 4
D_HALF = N_HEADS * LANES
CHUNK = LANES
U_SLABS = D_HALF // (2 * LANES)
ROPE_THETA = 10000.0
EPS = 1e-6
VMEM_LIMIT = 56 * 1024 * 1024
DFT_ROWS = 8
DFT_GROUPS = 2


def _params(n_axes=1, vmem=VMEM_LIMIT):
    return pltpu.CompilerParams(dimension_semantics=("arbitrary",) * n_axes, vmem_limit_bytes=vmem)


def _store_interleaved(ref, a, val):
    rows, n, lanes = ref.shape
    ref.reshape(rows * n, lanes)[pl.ds(a, rows, stride=n), :] = val


def _const_spec(shape):
    zeros = (0,) * len(shape)
    return pl.BlockSpec(shape, lambda *_: zeros, pipeline_mode=pl.Buffered(1))


def _layer_spec(stacked_shape, layer):
    index = (layer,) + (0,) * (len(stacked_shape) - 1)
    return pl.BlockSpec((None,) + tuple(stacked_shape[1:]), lambda *_: index, pipeline_mode=pl.Buffered(1))


def _pack_pair(lo, hi):
    return pltpu.pack_elementwise([lo, hi], packed_dtype=BF16)


def _unpack_pair(words):
    lo = pltpu.unpack_elementwise(words, index=0, packed_dtype=BF16, unpacked_dtype=F32)
    hi = pltpu.unpack_elementwise(words, index=1, packed_dtype=BF16, unpacked_dtype=F32)
    return lo, hi


PROJ_U, PROJ_Q, PROJ_K, PROJ_V, PROJ_G = (b * D_HALF for b in range(5))


def _prep_proj_kernel(cs_ref, win_ref, wf_ref, o_ref, m_ref):
    n = pl.program_id(1)

    @pl.when(n == 0)
    def _():
        for g in range(N_HEADS):
            cw = jnp.dot(cs_ref[...], wf_ref[0, g], precision=lax.Precision.HIGHEST, preferred_element_type=F32)
            m_ref[0, g] = jnp.concatenate([cw[:LANES], cw[LANES:]], axis=1).astype(BF16)

    @pl.when(n == PROJ_V // D_HALF)
    def _():
        for g in range(N_HEADS):
            cols = slice(g * LANES, (g + 1) * LANES)
            w = win_ref[0, :, cols]
            o_ref[0, :, cols] = (w - jnp.mean(w, axis=-1, keepdims=True)).astype(BF16)

    @pl.when(n != PROJ_V // D_HALF)
    def _():
        o_ref[0] = win_ref[0].astype(BF16)


def _prep_proj_weights(w_in, w_fmix):
    depth, d, width = w_in.shape
    c = np.arange(LANES)
    ang = 2.0 * np.pi * ((c[:, None] * c[None, :]) % LANES) / LANES
    cs = jnp.asarray(np.concatenate([np.cos(ang), -np.sin(ang)], axis=0), F32)
    return pl.pallas_call(
        _prep_proj_kernel,
        out_shape=(
            jax.ShapeDtypeStruct((depth, d, width), BF16),
            jax.ShapeDtypeStruct((depth, N_HEADS, LANES, 2 * LANES), BF16),
        ),
        grid=(depth, width // D_HALF),
        in_specs=[
            pl.BlockSpec((2 * LANES, LANES), lambda l, n: (0, 0)),
            pl.BlockSpec((1, d, D_HALF), lambda l, n: (l, 0, n)),
            pl.BlockSpec((1, N_HEADS, LANES, LANES), lambda l, n: (l, 0, 0, 0)),
        ],
        out_specs=(
            pl.BlockSpec((1, d, D_HALF), lambda l, n: (l, 0, n)),
            pl.BlockSpec((1, N_HEADS, LANES, 2 * LANES), lambda l, n: (l, 0, 0, 0)),
        ),
        compiler_params=_params(2),
        name="prep_proj_weights",
    )(cs, w_in, w_fmix)


def _rmsnorm(x, gain):
    return x * lax.rsqrt(jnp.mean(x * x, axis=-1, keepdims=True) + EPS) * gain


def _log_sigmoid(x):
    return -(jnp.maximum(-x, 0.0) + jnp.log1p(jnp.exp(-jnp.abs(x))))


def _proj_kernel(x_ref, ln_ref, w_ref, cos_ref, sin_ref, decf_ref, u_ref, q_ref, kt_ref, v_ref, g_ref, sf_ref,
                 carry_ref, *, tiles_per_seq):
    tm = x_ref.shape[0]

    @pl.when(pl.program_id(0) % tiles_per_seq == 0)
    def _():
        carry_ref[...] = jnp.zeros_like(carry_ref)

    h = _rmsnorm(x_ref[...], ln_ref[...]).astype(BF16)

    def project(first_col):
        return jnp.dot(h, w_ref[:, first_col:first_col + D_HALF], preferred_element_type=F32)

    cos = cos_ref[...]
    sin = sin_ref[...]

    def rope(t):
        return t * cos + pltpu.roll(t, LANES // 2, 1) * sin

    kf = project(PROJ_K)
    v = project(PROJ_V).astype(BF16)
    v_ref[...] = v
    qf = project(PROJ_Q)
    k_scale = LANES ** -0.5
    lgf = _log_sigmoid(decf_ref[...])
    lane = lax.broadcasted_iota(jnp.int32, (SUBLANES, LANES), 1).astype(F32)
    for hd in range(N_HEADS):
        cols = slice(hd * LANES, (hd + 1) * LANES)
        q_ref[:, cols] = rope(qf[:, cols]).astype(BF16)
        kt = (rope(kf[:, cols]) * k_scale).T
        k_decay = jnp.exp(lgf[hd] * (float(CHUNK - 1) - lane))[0:1, :]
        c_decay = jnp.exp(lgf[hd] * float(CHUNK))[0:1, :]
        for c in range(tm // CHUNK):
            toks = slice(c * CHUNK, (c + 1) * CHUNK)
            kt_ref[c, cols, :] = kt[:, toks].astype(BF16)
            state = carry_ref[hd]
            sf_ref[c, hd] = state.astype(BF16)
            kv = jnp.dot((kt[:, toks] * k_decay).astype(BF16), v[toks, cols], preferred_element_type=F32)
            carry_ref[hd] = state * c_decay + kv

    u = project(PROJ_U)
    uw = _pack_pair(u[:, :D_HALF // 2], u[:, D_HALF // 2:])
    for sl in range(u_ref.shape[0]):
        for a in range(tm // LANES):
            _store_interleaved(u_ref.at[sl, 0], a, uw[a * LANES:(a + 1) * LANES, sl * LANES:(sl + 1) * LANES])
    g_ref[...] = project(PROJ_G).astype(BF16)


def _proj(x, ln, w, layer, cos_t, sin_t, dec_f, *, seq):
    t, d = x.shape
    tm = SUBLANES * LANES
    pos_blocks = seq // tm
    n_chunks = tm // CHUNK
    row = lambda i: (i, 0)
    return pl.pallas_call(
        functools.partial(_proj_kernel, tiles_per_seq=pos_blocks),
        out_shape=(
            jax.ShapeDtypeStruct((U_SLABS, t // tm, LANES, tm // LANES, LANES), jnp.uint32),
            jax.ShapeDtypeStruct((t, D_HALF), BF16),
            jax.ShapeDtypeStruct((t // CHUNK, D_HALF, CHUNK), BF16),
            jax.ShapeDtypeStruct((t, D_HALF), BF16),
            jax.ShapeDtypeStruct((t, D_HALF), BF16),
            jax.ShapeDtypeStruct((t // CHUNK, N_HEADS, LANES, LANES), BF16),
        ),
        grid=(t // tm,),
        in_specs=[
            pl.BlockSpec((tm, d), row),
            _const_spec((1, d)),
            _layer_spec(w.shape, layer),
            pl.BlockSpec((tm, LANES), lambda i: (i % pos_blocks, 0)),
            pl.BlockSpec((tm, LANES), lambda i: (i % pos_blocks, 0)),
            _const_spec((N_HEADS, SUBLANES, LANES)),
        ],
        out_specs=(
            pl.BlockSpec((U_SLABS, 1, LANES, tm // LANES, LANES), lambda i: (0, i, 0, 0, 0)),
            pl.BlockSpec((tm, D_HALF), row),
            pl.BlockSpec((n_chunks, D_HALF, CHUNK), lambda i: (i, 0, 0)),
            pl.BlockSpec((tm, D_HALF), row),
            pl.BlockSpec((tm, D_HALF), row),
            pl.BlockSpec((n_chunks, N_HEADS, LANES, LANES), lambda i: (i, 0, 0, 0)),
        ),
        scratch_shapes=[pltpu.VMEM((N_HEADS, LANES, LANES), F32)],
        compiler_params=_params(1),
        name="proj",
    )(x, ln, w, cos_t, sin_t, dec_f)


RET_UNROLL = 16


def _ret_main_kernel(decf_ref, decb_ref, gain_ref, q_ref, kt_ref, v_ref, g_ref, sf_ref, o_ref,
                     mask_ref, qdf_ref, qdb_ref, gs_ref, kv_ref, sb_ref, carry_ref, *, blocks_per_seq):
    n_chunks = kt_ref.shape[0]
    blk = pl.num_programs(0) - 1 - pl.program_id(0)

    @pl.when(blk % blocks_per_seq == blocks_per_seq - 1)
    def _():
        carry_ref[...] = jnp.zeros_like(carry_ref)

    lgb = _log_sigmoid(decb_ref[...])
    lane = lax.broadcasted_iota(jnp.int32, (SUBLANES, LANES), 1).astype(F32)
    k_decay = [jnp.exp(lgb[hd] * lane)[0:1, :] for hd in range(N_HEADS)]
    c_decay = [jnp.exp(lgb[hd] * float(CHUNK))[0:1, :] for hd in range(N_HEADS)]

    def kv_step(i, carry):
        for u in range(RET_UNROLL):
            c = i * RET_UNROLL + u
            rows = pl.ds(pl.multiple_of(c * CHUNK, CHUNK), CHUNK)
            for hd in range(N_HEADS):
                cols = slice(hd * LANES, (hd + 1) * LANES)
                kd = (kt_ref[c, cols, :].astype(F32) * k_decay[hd]).astype(BF16)
                kv_ref[c, hd] = jnp.dot(kd, v_ref[rows, cols], preferred_element_type=F32)
        return carry

    lax.fori_loop(0, n_chunks // RET_UNROLL, kv_step, 0)

    def scan_step(i, carry):
        c = n_chunks - 1 - i
        for hd in range(N_HEADS):
            state = carry_ref[hd]
            sb_ref[c, hd] = state.astype(BF16)
            carry_ref[hd] = state * c_decay[hd] + kv_ref[c, hd]
        return carry

    lax.fori_loop(0, n_chunks, scan_step, 0)

    @pl.when(pl.program_id(0) == 0)
    def _():
        lgf = _log_sigmoid(decf_ref[...])
        lgb = _log_sigmoid(decb_ref[...])
        i = lax.broadcasted_iota(jnp.int32, (CHUNK, CHUNK), 0).astype(F32)
        j = lax.broadcasted_iota(jnp.int32, (CHUNK, CHUNK), 1).astype(F32)
        diff = i - j
        for hd in range(N_HEADS):
            lf = lgf[hd][0:1, :]
            lb = lgb[hd][0:1, :]
            fwd = jnp.where(diff >= 0, jnp.exp(lf * jnp.maximum(diff, 0.0)), 0.0)
            bwd = jnp.where(diff <= 0, jnp.exp(lb * jnp.maximum(-diff, 0.0)), 0.0)
            mask_ref[hd] = fwd + bwd
            qdf_ref[hd] = jnp.exp(lf * (i + 1.0)).astype(BF16)
            qdb_ref[hd] = jnp.exp(lb * (float(CHUNK) - i)).astype(BF16)
        gs_ref[...] = gain_ref[...] * (float(LANES) ** 0.5)

    def one_chunk(c):
        rows = pl.ds(pl.multiple_of(c * CHUNK, CHUNK), CHUNK)
        for hd in range(N_HEADS):
            cols = slice(hd * LANES, (hd + 1) * LANES)
            qh = q_ref[rows, cols]
            scores = jnp.dot(qh, kt_ref[c, cols, :], preferred_element_type=F32) * mask_ref[hd]
            lhs = jnp.concatenate([scores.astype(BF16), qh * qdf_ref[hd], qh * qdb_ref[hd]], axis=1)
            rhs = jnp.concatenate([v_ref[rows, cols], sf_ref[c, hd], sb_ref[c, hd]], axis=0)
            d = jnp.dot(lhs, rhs, preferred_element_type=F32)
            ss = jnp.sum(d * d, axis=-1, keepdims=True)
            yn = d * lax.rsqrt(ss + float(LANES) * EPS) * gs_ref[:, cols]
            gate = g_ref[rows, cols].astype(F32)
            o_ref[rows, cols] = (gate * jax.nn.sigmoid(gate) * yn).astype(BF16)

    def body(i, carry):
        for u in range(RET_UNROLL):
            one_chunk(i * RET_UNROLL + u)
        return carry

    lax.fori_loop(0, n_chunks // RET_UNROLL, body, 0)


def _ret_main(dec_f, dec_b, gain, q, kt, v, g, sf, *, seq, blk):
    t = v.shape[0]
    nb = t // blk
    n_chunks = blk // CHUNK
    assert n_chunks % RET_UNROLL == 0
    row = lambda j: (nb - 1 - j, 0)
    return pl.pallas_call(
        functools.partial(_ret_main_kernel, blocks_per_seq=seq // blk),
        out_shape=jax.ShapeDtypeStruct((t, D_HALF), BF16),
        grid=(nb,),
        in_specs=[
            _const_spec((N_HEADS, SUBLANES, LANES)),
            _const_spec((N_HEADS, SUBLANES, LANES)),
            _const_spec((1, D_HALF)),
            pl.BlockSpec((blk, D_HALF), row),
            pl.BlockSpec((n_chunks, D_HALF, CHUNK), lambda j: (nb - 1 - j, 0, 0)),
            pl.BlockSpec((blk, D_HALF), row),
            pl.BlockSpec((blk, D_HALF), row),
            pl.BlockSpec((n_chunks, N_HEADS, LANES, LANES), lambda j: (nb - 1 - j, 0, 0, 0)),
        ],
        out_specs=pl.BlockSpec((blk, D_HALF), row),
        scratch_shapes=[
            pltpu.VMEM((N_HEADS, CHUNK, CHUNK), F32),
            pltpu.VMEM((N_HEADS, CHUNK, CHUNK), BF16),
            pltpu.VMEM((N_HEADS, CHUNK, CHUNK), BF16),
            pltpu.VMEM((1, D_HALF), F32),
            pltpu.VMEM((n_chunks, N_HEADS, LANES, LANES), F32),
            pltpu.VMEM((n_chunks, N_HEADS, LANES, LANES), BF16),
            pltpu.VMEM((N_HEADS, LANES, LANES), F32),
        ],
        compiler_params=_params(1),
        name="ret_main",
    )(dec_f, dec_b, gain, q, kt, v, g, sf)


def _slab_rows(ref, j):
    slabs, blocks, _, tile, lanes = ref.shape
    return jnp.concatenate([ref[sl, :, j].reshape(blocks * tile, lanes) for sl in range(slabs)], axis=1)


def _dft_stage_a_kernel(u_ref, m_ref, g_ref, y_ref):
    r = u_ref.shape[1] * u_ref.shape[3]
    for grp in range(y_ref.shape[1]):
        s2 = [grp * DFT_ROWS + jj for jj in range(DFT_ROWS)]
        words = jnp.concatenate([_slab_rows(u_ref, j) for j in s2], axis=0)
        lo, hi = _unpack_pair(words)
        u = jnp.concatenate([lo, hi], axis=1).astype(BF16)
        zz = [jnp.dot(u[:, g * LANES:(g + 1) * LANES], m_ref[g], preferred_element_type=F32) for g in range(N_HEADS)]
        zr = jnp.concatenate([z[:, :LANES] for z in zz], axis=1).astype(BF16)
        zi = jnp.concatenate([z[:, LANES:] for z in zz], axis=1).astype(BF16)
        for jj, j in enumerate(s2):
            rows = slice(jj * r, (jj + 1) * r)
            rhs = jnp.concatenate([zr[rows], zi[rows]], axis=0)
            y = jnp.dot(g_ref[j], rhs, preferred_element_type=F32)
            yw = _pack_pair(y[:r], y[r:])
            for sl in range(N_HEADS):
                _store_interleaved(y_ref.at[sl, grp], jj, yw[:, sl * LANES:(sl + 1) * LANES])


def _dft_stage_b_kernel(y_ref, h_ref, o_ref):
    for grp in range(o_ref.shape[1]):
        for kk in range(DFT_ROWS):
            yr, yi = _unpack_pair(_slab_rows(y_ref, grp * DFT_ROWS + kk))
            rhs = jnp.concatenate([yr, yi], axis=0).astype(BF16)
            out = jnp.dot(h_ref[...], rhs, preferred_element_type=F32)
            ow = _pack_pair(out[:, :D_HALF // 2], out[:, D_HALF // 2:])
            for sl in range(U_SLABS):
                _store_interleaved(o_ref.at[sl, grp], kk, ow[:, sl * LANES:(sl + 1) * LANES])


def _dft_tables(seq, batch):
    n2 = LANES
    n1 = seq // n2
    r = batch * n1
    idx = np.arange(2 * r)
    part, seq_id, digit = idx // r, (idx % r) // n1, idx % n1
    theta_a = 2.0 * np.pi * ((digit[:, None] * digit[None, :]) % n1) / n1
    theta_a = theta_a + (part[:, None] - part[None, :]) * (np.pi / 2.0)
    same_seq = seq_id[:, None] == seq_id[None, :]
    theta_b = 2.0 * np.pi * (np.arange(n2)[:, None] * digit[None, :]) / seq
    a_cos, a_sin = (jnp.asarray(f(theta_a) * same_seq, F32)[None] for f in (np.cos, np.sin))
    b_cos, b_sin = (jnp.asarray(f(theta_b), F32)[:, :, None] for f in (np.cos, np.sin))
    g = (a_cos * b_cos - a_sin * b_sin).astype(BF16)
    k2 = np.arange(n2)
    ang2 = 2.0 * np.pi * ((k2[:, None] * k2[None, :]) % n2) / n2
    scale = (seq * LANES) ** -0.5
    hm = jnp.asarray(np.concatenate([np.cos(ang2), np.sin(ang2)], axis=1) * scale, F32).astype(BF16)
    return g, hm


def _dft_two_stage(u, m, layer, g, hm):
    _, r_blocks, n2, tile, _ = u.shape
    r = r_blocks * tile
    per_step = DFT_GROUPS * DFT_ROWS
    y = pl.pallas_call(
        _dft_stage_a_kernel,
        out_shape=jax.ShapeDtypeStruct((N_HEADS, n2 // DFT_ROWS, r, DFT_ROWS, LANES), jnp.uint32),
        grid=(n2 // per_step,),
        in_specs=[
            pl.BlockSpec((U_SLABS, r_blocks, per_step, tile, LANES), lambda j: (0, 0, j, 0, 0)),
            _layer_spec(m.shape, layer),
            pl.BlockSpec((per_step, 2 * r, 2 * r), lambda j: (j, 0, 0)),
        ],
        out_specs=pl.BlockSpec((N_HEADS, DFT_GROUPS, r, DFT_ROWS, LANES), lambda j: (0, j, 0, 0, 0)),
        compiler_params=_params(1),
        name="dft_stage_a",
    )(u, m, g)
    return pl.pallas_call(
        _dft_stage_b_kernel,
        out_shape=jax.ShapeDtypeStruct((U_SLABS, r // DFT_ROWS, n2, DFT_ROWS, LANES), jnp.uint32),
        grid=(r // per_step,),
        in_specs=[
            pl.BlockSpec((N_HEADS, n2 // DFT_ROWS, per_step, DFT_ROWS, LANES), lambda i: (0, 0, i, 0, 0)),
            _const_spec((n2, 2 * n2)),
        ],
        out_specs=pl.BlockSpec((U_SLABS, DFT_GROUPS, n2, DFT_ROWS, LANES), lambda i: (0, i, 0, 0, 0)),
        compiler_params=_params(1),
        name="dft_stage_b",
    )(y, hm)


FF_CHUNK = 1024


def _out_mlp_kernel(x_ref, fo_ref, ro_ref, wo_ref, ln2_ref, w1_ref, w2_ref, lnf_ref, *rest, final, n_cast):
    o_ref = rest[n_cast]
    for src_ref, dst_ref in zip(rest[:n_cast], rest[n_cast + 1:]):
        dst_ref[...] = src_ref[...].astype(BF16)
    slabs, k1_blocks, n_k2, _, _ = fo_ref.shape
    lo, hi = _unpack_pair(jnp.concatenate(
        [jnp.concatenate([fo_ref[sl, kb, k2] for k2 in range(n_k2) for kb in range(k1_blocks)], axis=0)
         for sl in range(slabs)], axis=1))
    fo = jnp.concatenate([lo, hi], axis=1).astype(BF16)
    x1 = x_ref[...] + jnp.dot(fo, wo_ref[:D_HALF], preferred_element_type=F32)
    x1 = x1 + jnp.dot(ro_ref[...], wo_ref[D_HALF:], preferred_element_type=F32)
    h = _rmsnorm(x1, ln2_ref[...]).astype(BF16)
    acc = None
    for c in range(w1_ref.shape[1] // FF_CHUNK):
        cols = slice(c * FF_CHUNK, (c + 1) * FF_CHUNK)
        a = jnp.dot(h, w1_ref[:, cols], preferred_element_type=F32)
        a = jnp.square(jnp.maximum(a, 0.0)).astype(BF16)
        part = jnp.dot(a, w2_ref[cols, :], preferred_element_type=F32)
        acc = part if acc is None else acc + part
    acc = acc + x1
    if final:
        acc = _rmsnorm(acc, lnf_ref[...])
    o_ref[...] = acc


def _out_mlp(x, fo, ro, weights, ln2, lnf, *, seq, final, cast_next=None):
    t, d = x.shape
    tm = 512
    steps = t // tm
    n1 = seq // LANES
    tiles_per_seq = seq // tm
    k1_blocks = n1 // DFT_ROWS
    fo = fo.reshape(U_SLABS, t // seq, k1_blocks, LANES, DFT_ROWS, LANES)
    row = lambda i: (i, 0)
    wo, w1, w2 = weights
    cast_args, cast_in_specs, cast_out_specs, cast_out_shapes = [], [], [], []
    if cast_next is not None:
        stacked, nxt = cast_next
        for w in stacked:
            rows, cols = w.shape[1] // steps, w.shape[2]
            cast_args.append(w)
            cast_in_specs.append(pl.BlockSpec((None, rows, cols), lambda i: (nxt, i, 0)))
            cast_out_specs.append(pl.BlockSpec((rows, cols), row))
            cast_out_shapes.append(jax.ShapeDtypeStruct(w.shape[1:], BF16))
    out = pl.pallas_call(
        functools.partial(_out_mlp_kernel, final=final, n_cast=len(cast_args)),
        out_shape=[jax.ShapeDtypeStruct((t, d), F32)] + cast_out_shapes,
        grid=(steps,),
        in_specs=[
            pl.BlockSpec((tm, d), row),
            pl.BlockSpec((U_SLABS, None, k1_blocks, tm // n1, DFT_ROWS, LANES),
                         lambda i: (0, i // tiles_per_seq, 0, i % tiles_per_seq, 0, 0)),
            pl.BlockSpec((tm, D_HALF), row),
            _const_spec(wo.shape),
            _const_spec((1, d)),
            _const_spec(w1.shape),
            _const_spec(w2.shape),
            _const_spec((1, d)),
        ] + cast_in_specs,
        out_specs=[pl.BlockSpec((tm, d), row)] + cast_out_specs,
        compiler_params=_params(1),
        name="out_mlp",
    )(x, fo, ro, wo, ln2, w1, w2, lnf, *cast_args)
    return out[0], tuple(out[1:])


def _rope_tables(seq):
    half = LANES // 2
    inv = ROPE_THETA ** (-np.arange(half, dtype=np.float64) / half)
    ang = np.arange(seq, dtype=np.float64)[:, None] * inv[None, :]
    c, s = np.cos(ang), np.sin(ang)
    cos_t = np.concatenate([c, c], axis=1).astype(np.float32)
    sin_t = np.concatenate([-s, s], axis=1).astype(np.float32)
    return jnp.asarray(cos_t), jnp.asarray(sin_t)


def _bcast_heads(v):
    return jnp.broadcast_to(v.astype(F32)[:, None, None], (N_HEADS, SUBLANES, LANES))


def _trunk(x, weights, mlp_weights, mlp_f32, *, seq):
    (wproj, m_dft), ln1, ln2, dec_f, dec_b, gain, lnf = weights
    depth = wproj.shape[0]
    cos_t, sin_t = _rope_tables(seq)
    g_dft, h_dft = _dft_tables(seq, x.shape[0] // seq)
    blk = min(seq, 2048)
    mlp_weights = list(mlp_weights)
    for l in range(depth):
        df, db = _bcast_heads(dec_f[l]), _bcast_heads(dec_b[l])
        u, q, kt, v, g, sf = _proj(x, ln1[l][None], wproj, l, cos_t, sin_t, df, seq=seq)
        ro = _ret_main(df, db, gain[l][None].astype(F32), q, kt, v, g, sf, seq=seq, blk=blk)
        fo = _dft_two_stage(u, m_dft, l, g_dft, h_dft)
        cast_next = (mlp_f32, l + 1) if len(mlp_weights) == l + 1 and l + 1 < depth else None
        x, nxt = _out_mlp(x, fo, ro, mlp_weights[l], ln2[l][None], lnf[None], seq=seq, final=(l == depth - 1),
                          cast_next=cast_next)
        if cast_next is not None:
            mlp_weights.append(nxt)
    return x, mlp_weights


def kernel(x_prompt, x_sample, ln1, w_in, w_fmix, decay_fwd, decay_bwd, gn_gain, w_o, ln2, w_ff1, w_ff2, ln_final):
    d = x_prompt.shape[-1]
    weights = (_prep_proj_weights(w_in, w_fmix), ln1, ln2, decay_fwd, decay_bwd, gn_gain, ln_final)
    mlp_f32 = (w_o, w_ff1, w_ff2)
    first = [tuple(w[0].astype(BF16) for w in mlp_f32)]
    y_prompt, mlp_weights = _trunk(x_prompt.reshape(-1, d), weights, first, mlp_f32, seq=x_prompt.shape[1])
    y_sample, _ = _trunk(x_sample.reshape(-1, d), weights, mlp_weights, mlp_f32, seq=x_sample.shape[1])
    return y_prompt.reshape(x_prompt.shape), y_sample.reshape(x_sample.shape)
```
